```python
import jax, jax.numpy as jnp
from jax import lax
import numpy as np

D_MODEL = 2048
BATCH = 2
SEQ = 8192
DEPTH = 1
DEC_BATCH = 4
DEC_SEQ = 8192
PAST_LEN = 128

N_META = 16
RMS_EPS = 1e-6
MLA_HEADS = 8
QK_NOPE = 128
QK_ROPE = 64
QK_HEAD = QK_NOPE + QK_ROPE
V_HEAD = 128
Q_LORA = 512
KV_LORA = 512
ROPE_THETA = 10000.0
Q_BLOCK = 128
MLA_WIDTH = MLA_HEADS * V_HEAD
RWKV_HEADS = 16
RWKV_HEAD = 64
RWKV_WIDTH = RWKV_HEADS * RWKV_HEAD
DECAY_LORA = 64
ICL_LORA = 64
GATE_LORA = 128
GN_EPS = 64e-5
RWKV_SPLITS = (RWKV_WIDTH, RWKV_WIDTH, RWKV_WIDTH, DECAY_LORA, DECAY_LORA, ICL_LORA, ICL_LORA, GATE_LORA)
RWKV_COLS = sum(RWKV_SPLITS)
IN_SPLITS = (Q_LORA, KV_LORA, QK_ROPE, RWKV_COLS, D_MODEL, D_MODEL)
IN_COLS = sum(IN_SPLITS)
N_GROUPS = 4
EXPERTS_PER_GROUP = 8
N_EXPERTS = N_GROUPS * EXPERTS_PER_GROUP
TOP_K = 2
D_EXPERT = 512

kernel_name = 'hybrid_mla_rwkv7_hmoe_encoder'


def _rms_norm(x, g, eps=RMS_EPS):
    xf = x.astype(jnp.float32)
    y = xf * lax.rsqrt(jnp.mean(xf * xf, axis=-1, keepdims=True) + eps)
    return (y * g.astype(jnp.float32)).astype(x.dtype)


def _offsets(sizes):
    return np.cumsum(sizes)[:-1].tolist()


def _rope_tables(length):
    pos = jnp.arange(length, dtype=jnp.float32)
    inv = ROPE_THETA ** (-jnp.arange(0, QK_ROPE, 2, dtype=jnp.float32) / QK_ROPE)
    ang = pos[:, None] * inv[None, :]
    return jnp.cos(ang), jnp.sin(ang)


def _rope_tail(x, cos, sin):
    x_pass, x_rot = jnp.split(x, [QK_NOPE], axis=-1)
    x1, x2 = jnp.split(x_rot.astype(jnp.float32), 2, axis=-1)
    c = cos[None, :, None, :]
    s = sin[None, :, None, :]
    rot = jnp.concatenate([x1 * c - x2 * s, x2 * c + x1 * s], axis=-1).astype(x.dtype)
    return jnp.concatenate([x_pass, rot], axis=-1)


def _block_attention(q, k, v):
    b, length, h, dq = q.shape
    n_blk = -(-length // Q_BLOCK)
    pad = n_blk * Q_BLOCK - length
    qp = jnp.pad(q, ((0, 0), (0, pad), (0, 0), (0, 0)))
    qb = qp.reshape(b, n_blk, Q_BLOCK, h, dq).transpose(1, 0, 2, 3, 4)
    scale = dq ** -0.5

    def one_block(q_blk):
        s = jnp.einsum('bqhd,bkhd->bhqk', q_blk, k).astype(jnp.float32) * scale
        p = jax.nn.softmax(s, axis=-1)
        return jnp.einsum('bhqk,bkhd->bqhd', p.astype(v.dtype), v)

    o = lax.map(one_block, qb)
    o = o.transpose(1, 0, 2, 3, 4).reshape(b, n_blk * Q_BLOCK, h, v.shape[-1])
    return o[:, :length]


def _mla(c_q, c_kv, k_r, cos, sin, q_a_norm, w_q_b, kv_a_norm, w_kv_b, q_norm, k_norm):
    b, length = c_q.shape[:2]
    q = (_rms_norm(c_q, q_a_norm) @ w_q_b).reshape(b, length, MLA_HEADS, QK_HEAD)
    kv = (_rms_norm(c_kv, kv_a_norm) @ w_kv_b).reshape(b, length, MLA_HEADS, QK_NOPE + V_HEAD)
    k_nope, v = jnp.split(kv, [QK_NOPE], axis=-1)
    k_rope = jnp.broadcast_to(k_r[:, :, None, :], (b, length, MLA_HEADS, QK_ROPE))
    k = jnp.concatenate([k_nope, k_rope], axis=-1)
    q = _rope_tail(_rms_norm(q, q_norm), cos, sin)
    k = _rope_tail(_rms_norm(k, k_norm), cos, sin)
    o = _block_attention(q, k, v)
    return o.reshape(b, length, MLA_WIDTH)


def _wkv7_scan(r, w, k, v, kk, a, reverse):
    b, _, h, n = r.shape
    xs = tuple(jnp.moveaxis(t, 1, 0) for t in (r, w, k, v, kk, a))

    def step(state, inp):
        r_t, w_t, k_t, v_t, kk_t, a_t = inp
        sa = jnp.einsum('bhij,bhj->bhi', state, -kk_t)
        state = (state * w_t[:, :, None, :]
                 + sa[..., None] * (kk_t * a_t)[:, :, None, :]
                 + v_t[..., None] * k_t[:, :, None, :])
        return state, jnp.einsum('bhij,bhj->bhi', state, r_t)

    s0 = jnp.zeros((b, h, n, n), jnp.float32)
    _, ys = lax.scan(step, s0, xs, reverse=reverse)
    return jnp.moveaxis(ys, 0, 1)


def _rwkv7(z, shift_prev, shift_next, w0_f, w2_f, w0_b, w2_b, a0_f, a2_f, a0_b, a2_b,
           g2, k_k, k_a, r_k, ln_w, ln_b):
    f32 = jnp.float32
    bsz, length, _ = z.shape
    z_prev = jnp.pad(z, ((0, 0), (1, 0), (0, 0)))[:, :-1]
    z_next = jnp.pad(z, ((0, 0), (0, 1), (0, 0)))[:, 1:]
    z = z + shift_prev * (z_prev - z) + shift_next * (z_next - z)
    r, k, v, wd_f, wd_b, ad_f, ad_b, gd = jnp.split(z, _offsets(RWKV_SPLITS), axis=-1)

    def heads(t):
        return t.astype(f32).reshape(bsz, length, RWKV_HEADS, RWKV_HEAD)

    rh = heads(r)
    vh = heads(v)
    kk = heads(k * k_k)
    kk = kk / jnp.maximum(jnp.sqrt(jnp.sum(kk * kk, axis=-1, keepdims=True)), 1e-12)
    rk = r_k.astype(f32)

    def direction(wd, w0, w2, ad, a0, a2, reverse):
        w_raw = (w0 + jnp.tanh(wd) @ w2).astype(f32)
        decay = jnp.exp(-jnp.exp(-jax.nn.softplus(-w_raw) - 0.5))
        icl = jax.nn.sigmoid((a0 + ad @ a2).astype(f32))
        k_rep = k.astype(f32) * (1.0 + (icl - 1.0) * k_a.astype(f32))
        kh = heads(k_rep)
        y = _wkv7_scan(rh, heads(decay), kh, vh, kk, heads(icl), reverse)
        bonus = jnp.sum(rh * kh * rk, axis=-1, keepdims=True) * vh
        return y, bonus

    y_f, bon_f = direction(wd_f, w0_f, w2_f, ad_f, a0_f, a2_f, False)
    y_b, bon_b = direction(wd_b, w0_b, w2_b, ad_b, a0_b, a2_b, True)
    y = y_f + y_b
    mu = jnp.mean(y, axis=-1, keepdims=True)
    var = jnp.mean(jnp.square(y - mu), axis=-1, keepdims=True)
    y = ((y - mu) * lax.rsqrt(var + GN_EPS)).reshape(bsz, length, RWKV_WIDTH)
    y = y * ln_w.astype(f32) + ln_b.astype(f32) + (bon_f + bon_b).reshape(bsz, length, RWKV_WIDTH)
    g = (jax.nn.sigmoid(gd) @ g2).astype(f32)
    return (y * g).astype(z.dtype)


def _moe_seq(t, w_rg, b_rg, w_re, b_re, w_gate, w_up, w_down):
    f32 = jnp.float32
    pg = jax.nn.softmax((t @ w_rg).astype(f32) + b_rg.astype(f32), axis=-1)
    pg_top, gi = lax.top_k(pg, 1)
    le = jnp.einsum('ld,gde->lge', t, w_re).astype(f32) + b_re.astype(f32)
    sel = jax.nn.one_hot(gi[:, 0], N_GROUPS, dtype=f32)
    le = jnp.sum(le * sel[:, :, None], axis=1)
    pe = jax.nn.softmax(le, axis=-1)
    pe_top, ei = lax.top_k(pe, TOP_K)
    pe_top = pe_top / jnp.sum(pe_top, axis=-1, keepdims=True)
    eid = gi * EXPERTS_PER_GROUP + ei
    gate = pg_top * pe_top
    combine = jnp.sum(jax.nn.one_hot(eid, N_EXPERTS, dtype=f32) * gate[..., None], axis=1)
    hg = jnp.einsum('ld,edf->lef', t, w_gate)
    hu = jnp.einsum('ld,edf->lef', t, w_up)
    act = jax.nn.silu(hg) * hu * combine[:, :, None].astype(t.dtype)
    return jnp.einsum('lef,efd->ld', act, w_down)


def _trunk(x, meta_tokens, p):
    b, s, _ = x.shape
    meta = jnp.broadcast_to(meta_tokens.astype(x.dtype)[None], (b, N_META, D_MODEL))
    h = jnp.concatenate([meta, x], axis=1)
    cos, sin = _rope_tables(s + N_META)
    for l in range(DEPTH):
        n = _rms_norm(h, p['attn_norm'][l])
        z = n @ p['w_in'][l]
        c_q, c_kv, k_r, z_rw, g_a, g_b = jnp.split(z, _offsets(IN_SPLITS), axis=-1)
        o_a = _mla(c_q, c_kv, k_r, cos, sin, p['q_a_norm'][l], p['w_q_b'][l],
                   p['kv_a_norm'][l], p['w_kv_b'][l], p['q_norm'][l], p['k_norm'][l])
        o_b = _rwkv7(z_rw, p['shift_prev'][l], p['shift_next'][l],
                     p['decay_w0_fwd'][l], p['decay_w2_fwd'][l], p['decay_w0_bwd'][l], p['decay_w2_bwd'][l],
                     p['icl_a0_fwd'][l], p['icl_a2_fwd'][l], p['icl_a0_bwd'][l], p['icl_a2_bwd'][l],
                     p['gate_w2'][l], p['k_k'][l], p['k_a'][l], p['r_k'][l], p['ln_x_w'][l], p['ln_x_b'][l])
        merged = (jax.nn.sigmoid(g_a) * (o_a @ p['w_branch_mla'][l])
                  + jax.nn.sigmoid(g_b) * (o_b @ p['w_branch_rwkv'][l]))
        h = h + merged @ p['w_out'][l]
        n2 = _rms_norm(h, p['ffn_norm'][l])
        w_rg, b_rg = p['w_router_group'][l], p['b_router_group'][l]
        w_re, b_re = p['w_router_expert'][l], p['b_router_expert'][l]
        w_gate, w_up, w_down = p['w_expert_gate'][l], p['w_expert_up'][l], p['w_expert_down'][l]
        ffn = lax.map(lambda t: _moe_seq(t, w_rg, b_rg, w_re, b_re, w_gate, w_up, w_down), n2)
        h = h + ffn
    return h[:, N_META:]


def setup_inputs(seed: int = 0) -> dict:
    key = jax.random.key(seed)
    keys = iter(jax.random.split(key, 48))

    def normal(shape, scale):
        return jax.random.normal(next(keys), shape, jnp.float32) * scale

    def uniform(shape, lo, hi):
        return jax.random.uniform(next(keys), shape, jnp.float32, lo, hi)

    def gain(shape):
        return 1.0 + normal(shape, 0.02)

    dl = DEPTH
    return {
        'x_prompt': normal((BATCH, SEQ, D_MODEL), 1.0),
        'x_sample': normal((DEC_BATCH, DEC_SEQ, D_MODEL), 1.0),
        'meta_tokens': normal((N_META, D_MODEL), 1.0),
        'attn_norm': gain((dl, D_MODEL)),
        'w_in': normal((dl, D_MODEL, IN_COLS), D_MODEL ** -0.5),
        'q_a_norm': gain((dl, Q_LORA)),
        'w_q_b': normal((dl, Q_LORA, MLA_HEADS * QK_HEAD), Q_LORA ** -0.5),
        'kv_a_norm': gain((dl, KV_LORA)),
        'w_kv_b': normal((dl, KV_LORA, MLA_HEADS * (QK_NOPE + V_HEAD)), KV_LORA ** -0.5),
        'q_norm': gain((dl, QK_HEAD)),
        'k_norm': gain((dl, QK_HEAD)),
        'shift_prev': uniform((dl, RWKV_COLS), 0.1, 0.45),
        'shift_next': uniform((dl, RWKV_COLS), 0.1, 0.45),
        'decay_w0_fwd': uniform((dl, RWKV_WIDTH), -6.0, 1.0),
        'decay_w2_fwd': normal((dl, DECAY_LORA, RWKV_WIDTH), 0.1),
        'decay_w0_bwd': uniform((dl, RWKV_WIDTH), -6.0, 1.0),
        'decay_w2_bwd': normal((dl, DECAY_LORA, RWKV_WIDTH), 0.1),
        'icl_a0_fwd': normal((dl, RWKV_WIDTH), 0.1),
        'icl_a2_fwd': normal((dl, ICL_LORA, RWKV_WIDTH), 0.5 * ICL_LORA ** -0.5),
        'icl_a0_bwd': normal((dl, RWKV_WIDTH), 0.1),
        'icl_a2_bwd': normal((dl, ICL_LORA, RWKV_WIDTH), 0.5 * ICL_LORA ** -0.5),
        'gate_w2': normal((dl, GATE_LORA, RWKV_WIDTH), GATE_LORA ** -0.5),
        'k_k': 0.85 + normal((dl, RWKV_WIDTH), 0.1),
        'k_a': 1.0 + normal((dl, RWKV_WIDTH), 0.1),
        'r_k': normal((dl, RWKV_HEADS, RWKV_HEAD), 0.1),
        'ln_x_w': gain((dl, RWKV_WIDTH)),
        'ln_x_b': normal((dl, RWKV_WIDTH), 0.02),
        'w_branch_mla': normal((dl, MLA_WIDTH, D_MODEL), MLA_WIDTH ** -0.5),
        'w_branch_rwkv': normal((dl, RWKV_WIDTH, D_MODEL), RWKV_WIDTH ** -0.5),
        'w_out': normal((dl, D_MODEL, D_MODEL), D_MODEL ** -0.5),
        'ffn_norm': gain((dl, D_MODEL)),
        'w_router_group': normal((dl, D_MODEL, N_GROUPS), D_MODEL ** -0.5),
        'b_router_group': normal((dl, N_GROUPS), 0.01),
        'w_router_expert': normal((dl, N_GROUPS, D_MODEL, EXPERTS_PER_GROUP), D_MODEL ** -0.5),
        'b_router_expert': normal((dl, N_GROUPS, EXPERTS_PER_GROUP), 0.01),
        'w_expert_gate': normal((dl, N_EXPERTS, D_MODEL, D_EXPERT), D_MODEL ** -0.5),
        'w_expert_up': normal((dl, N_EXPERTS, D_MODEL, D_EXPERT), D_MODEL ** -0.5),
        'w_expert_down': normal((dl, N_EXPERTS, D_EXPERT, D_MODEL), D_EXPERT ** -0.5),
    }


def reference(x_prompt, x_sample, meta_tokens, attn_norm, w_in, q_a_norm, w_q_b, kv_a_norm, w_kv_b,
              q_norm, k_norm, shift_prev, shift_next, decay_w0_fwd, decay_w2_fwd, decay_w0_bwd,
              decay_w2_bwd, icl_a0_fwd, icl_a2_fwd, icl_a0_bwd, icl_a2_bwd, gate_w2, k_k, k_a, r_k,
              ln_x_w, ln_x_b, w_branch_mla, w_branch_rwkv, w_out, ffn_norm, w_router_group,
              b_router_group, w_router_expert, b_router_expert, w_expert_gate, w_expert_up,
              w_expert_down):
    p = {
        'attn_norm': attn_norm, 'w_in': w_in, 'q_a_norm': q_a_norm, 'w_q_b': w_q_b,
        'kv_a_norm': kv_a_norm, 'w_kv_b': w_kv_b, 'q_norm': q_norm, 'k_norm': k_norm,
        'shift_prev': shift_prev, 'shift_next': shift_next,
        'decay_w0_fwd': decay_w0_fwd, 'decay_w2_fwd': decay_w2_fwd,
        'decay_w0_bwd': decay_w0_bwd, 'decay_w2_bwd': decay_w2_bwd,
        'icl_a0_fwd': icl_a0_fwd, 'icl_a2_fwd': icl_a2_fwd,
        'icl_a0_bwd': icl_a0_bwd, 'icl_a2_bwd': icl_a2_bwd,
        'gate_w2': gate_w2, 'k_k': k_k, 'k_a': k_a, 'r_k': r_k, 'ln_x_w': ln_x_w, 'ln_x_b': ln_x_b,
        'w_branch_mla': w_branch_mla, 'w_branch_rwkv': w_branch_rwkv, 'w_out': w_out,
        'ffn_norm': ffn_norm, 'w_router_group': w_router_group, 'b_router_group': b_router_group,
        'w_router_expert': w_router_expert, 'b_router_expert': b_router_expert,
        'w_expert_gate': w_expert_gate, 'w_expert_up': w_expert_up, 'w_expert_down': w_expert_down,
    }
    y_prompt = _trunk(x_prompt, meta_tokens, p)
    y_sample = _trunk(x_sample, meta_tokens, p)
    return (y_prompt, y_sample)
```

```python
import functools
import math

import jax
import jax.numpy as jnp
from jax import lax
from jax.experimental import pallas as pl
from jax.experimental.pallas import tpu as pltpu

F32 = jnp.float32
BF16 = jnp.bfloat16

N_META = 16
RMS_EPS = 1e-6
MLA_HEADS = 8
QK_NOPE = 128
QK_ROPE = 64
QK_HEAD = QK_NOPE + QK_ROPE
V_HEAD = 128
Q_LORA = 512
KV_LORA = 512
ROPE_THETA = 10000.0
RWKV_HEADS = 16
RWKV_HEAD = 64
RWKV_WIDTH = RWKV_HEADS * RWKV_HEAD
GN_EPS = 64e-5
N_GROUPS = 4
EXPERTS_PER_GROUP = 8
N_EXPERTS = N_GROUPS * EXPERTS_PER_GROUP
D_EXPERT = 512

LANES = 128
CHUNK = 64
META_ROWS = 128
NEG_BIG = -1e30
MIB = 1024 * 1024


def _params(sem, vmem_mib):
    return pltpu.CompilerParams(dimension_semantics=sem, vmem_limit_bytes=vmem_mib * MIB)


def _dot(a, b):
    return jnp.dot(a, b, preferred_element_type=F32)


def _dot_nt(a, b):
    return lax.dot_general(a, b, (((1,), (1,)), ((), ())), preferred_element_type=F32)


def _dot_tn(a, b):
    return lax.dot_general(a, b, (((0,), (0,)), ((), ())), preferred_element_type=F32)


def _split3(x):
    h1 = x.astype(BF16)
    r1 = x - h1.astype(F32)
    h2 = r1.astype(BF16)
    h3 = (r1 - h2.astype(F32)).astype(BF16)
    return h1, h2, h3


def _dot_exact_lhs(a_bf16, x):
    h1, h2, h3 = _split3(x)
    return _dot(a_bf16, h1) + _dot(a_bf16, h2) + _dot(a_bf16, h3)


def _dot_exact_rhs(x, b_bf16):
    h1, h2, h3 = _split3(x)
    return _dot(h1, b_bf16) + _dot(h2, b_bf16) + _dot(h3, b_bf16)


def _dot_f32(x, w):
    x1, x2, x3 = _split3(x)
    w1, w2, w3 = _split3(w)
    return (_dot(x1, w1) + (_dot(x1, w2) + _dot(x2, w1))
            + (_dot(x1, w3) + _dot(x2, w2) + _dot(x3, w1)))


def _norm_matmul_kernel(x_ref, g_ref, w_ref, o_ref, nb_ref):
    @pl.when(pl.program_id(1) == 0)
    def _():
        x = x_ref[...]
        y = x * lax.rsqrt(jnp.mean(x * x, axis=-1, keepdims=True) + RMS_EPS)
        nb_ref[...] = (y * g_ref[...]).astype(BF16)

    o_ref[...] = _dot(nb_ref[...], w_ref[...])


def _norm_matmul(x, gain, w, tn, tm=512):
    m, d = x.shape
    n = w.shape[1]
    tm = min(tm, m)
    assert m % tm == 0 and n % tn == 0
    return pl.pallas_call(
        _norm_matmul_kernel,
        out_shape=jax.ShapeDtypeStruct((m, n), F32),
        grid=(m // tm, n // tn),
        in_specs=[
            pl.BlockSpec((tm, d), lambda i, j: (i, 0)),
            pl.BlockSpec((1, d), lambda i, j: (0, 0)),
            pl.BlockSpec((d, tn), lambda i, j: (0, j)),
        ],
        out_specs=pl.BlockSpec((tm, tn), lambda i, j: (i, j)),
        scratch_shapes=[pltpu.VMEM((tm, d), BF16)],
        compiler_params=_params(("parallel", "arbitrary"), 48),
        name="norm_matmul",
    )(x, gain.reshape(1, d).astype(F32), w)


def _mla_prep_kernel(z_ref, cos_ref, sin_ref, qa_ref, kva_ref, wq_ref, wkv_ref, qn_ref, kn_ref,
                     q_ref, k_ref, v_ref):
    z = z_ref[...]
    cq = z[:, :Q_LORA]
    ckv = z[:, Q_LORA:Q_LORA + KV_LORA]
    kr = z[:, Q_LORA + KV_LORA:]
    cos = cos_ref[...]
    sin = sin_ref[...]

    def rms(x, g):
        return x * lax.rsqrt(jnp.mean(x * x, axis=-1, keepdims=True) + RMS_EPS) * g

    def rope(x):
        swapped = jnp.concatenate([x[:, QK_ROPE // 2:], x[:, :QK_ROPE // 2]], axis=-1)
        return x * cos + swapped * sin

    q_all = _dot(rms(cq, qa_ref[...]).astype(BF16), wq_ref[...])
    kv_all = _dot(rms(ckv, kva_ref[...]).astype(BF16), wkv_ref[...])
    qn = qn_ref[...]
    kn = kn_ref[...]
    scale = QK_HEAD ** -0.5
    kr_sq = jnp.sum(kr * kr, axis=-1, keepdims=True)
    kr_rot = rope(kr * kn[:, QK_NOPE:])
    for h in range(MLA_HEADS):
        qh = q_all[:, h * QK_HEAD:(h + 1) * QK_HEAD]
        qh = qh * (lax.rsqrt(jnp.mean(qh * qh, axis=-1, keepdims=True) + RMS_EPS) * scale) * qn
        q_ref[0, h, :, :QK_NOPE] = qh[:, :QK_NOPE].astype(BF16)
        q_ref[0, h, :, QK_NOPE:] = rope(qh[:, QK_NOPE:]).astype(BF16)
        base = h * (QK_NOPE + V_HEAD)
        k_nope = kv_all[:, base:base + QK_NOPE]
        ms = (jnp.sum(k_nope * k_nope, axis=-1, keepdims=True) + kr_sq) * (1.0 / QK_HEAD)
        inv = lax.rsqrt(ms + RMS_EPS)
        k_ref[0, h, :, :QK_NOPE] = (k_nope * inv * kn[:, :QK_NOPE]).astype(BF16)
        k_ref[0, h, :, QK_NOPE:] = (kr_rot * inv).astype(BF16)
        v_ref[0, h, :, :] = kv_all[:, base + QK_NOPE:base + QK_NOPE + V_HEAD].astype(BF16)


def _mla_prep(z_mla, batch, seq, cos, sin, qa, kva, wq, wkv, qn, kn, tm=512):
    tm = min(tm, seq)
    assert seq % tm == 0
    nt = seq // tm
    zc = z_mla.shape[1]
    full = lambda b, i: (0, 0)
    return pl.pallas_call(
        _mla_prep_kernel,
        out_shape=(
            jax.ShapeDtypeStruct((batch, MLA_HEADS, seq, QK_HEAD), BF16),
            jax.ShapeDtypeStruct((batch, MLA_HEADS, seq, QK_HEAD), BF16),
            jax.ShapeDtypeStruct((batch, MLA_HEADS, seq, V_HEAD), BF16),
        ),
        grid=(batch, nt),
        in_specs=[
            pl.BlockSpec((tm, zc), lambda b, i: (b * nt + i, 0)),
            pl.BlockSpec((tm, QK_ROPE), lambda b, i: (i, 0)),
            pl.BlockSpec((tm, QK_ROPE), lambda b, i: (i, 0)),
            pl.BlockSpec((1, Q_LORA), full),
            pl.BlockSpec((1, KV_LORA), full),
            pl.BlockSpec(wq.shape, full),
            pl.BlockSpec(wkv.shape, full),
            pl.BlockSpec((1, QK_HEAD), full),
            pl.BlockSpec((1, QK_HEAD), full),
        ],
        out_specs=(
            pl.BlockSpec((1, MLA_HEADS, tm, QK_HEAD), lambda b, i: (b, 0, i, 0)),
            pl.BlockSpec((1, MLA_HEADS, tm, QK_HEAD), lambda b, i: (b, 0, i, 0)),
            pl.BlockSpec((1, MLA_HEADS, tm, V_HEAD), lambda b, i: (b, 0, i, 0)),
        ),
        compiler_params=_params(("parallel", "parallel"), 48),
        name="mla_prep",
    )(z_mla, cos, sin, qa, kva, wq, wkv, qn, kn)


def _attention_kernel(q_ref, k_ref, v_ref, km_ref, vm_ref, o_ref, *, tk):
    q = q_ref[0, 0]
    tq = q.shape[0]
    seq = k_ref.shape[2]

    s = _dot_nt(q, km_ref[0])
    lane = lax.broadcasted_iota(jnp.int32, s.shape, 1)
    s = jnp.where(lane < N_META, s, NEG_BIG)
    m = jnp.max(s, axis=-1, keepdims=True)
    p = jnp.exp(s - m)
    l = jnp.sum(p, axis=-1, keepdims=True)
    acc = _dot(p.astype(BF16), vm_ref[0])

    def body(c, carry):
        m, l, acc = carry
        off = pl.multiple_of(c * tk, tk)
        k = k_ref[0, 0, pl.ds(off, tk), :]
        v = v_ref[0, 0, pl.ds(off, tk), :]
        s = _dot_nt(q, k)
        m_new = jnp.maximum(m, jnp.max(s, axis=-1, keepdims=True))
        alpha = jnp.exp(m - m_new)
        p = jnp.exp(s - m_new)
        l = alpha * l + jnp.sum(p, axis=-1, keepdims=True)
        acc = alpha * acc + _dot(p.astype(BF16), v)
        return m_new, l, acc

    m, l, acc = lax.fori_loop(0, seq // tk, body, (m, l, acc))
    o_ref[0] = (acc / l).astype(BF16)


def _attention(q, k, v, k_meta, v_meta, tq=512, tk=1024):
    batch, heads, seq, _ = q.shape
    tq = min(tq, seq)
    tk = min(tk, seq)
    assert seq % tq == 0 and seq % tk == 0
    return pl.pallas_call(
        functools.partial(_attention_kernel, tk=tk),
        out_shape=jax.ShapeDtypeStruct((batch, seq, heads * V_HEAD), BF16),
        grid=(batch, heads, seq // tq),
        in_specs=[
            pl.BlockSpec((1, 1, tq, QK_HEAD), lambda b, h, i: (b, h, i, 0)),
            pl.BlockSpec((1, 1, seq, QK_HEAD), lambda b, h, i: (b, h, 0, 0)),
            pl.BlockSpec((1, 1, seq, V_HEAD), lambda b, h, i: (b, h, 0, 0)),
            pl.BlockSpec((1, META_ROWS, QK_HEAD), lambda b, h, i: (h, 0, 0)),
            pl.BlockSpec((1, META_ROWS, V_HEAD), lambda b, h, i: (h, 0, 0)),
        ],
        out_specs=pl.BlockSpec((1, tq, V_HEAD), lambda b, h, i: (b, i, h)),
        compiler_params=_params(("parallel", "parallel", "arbitrary"), 48),
        name="mla_attention",
    )(q, k, v, k_meta, v_meta)


def _head_sum(x, ones_blk):
    cols = []
    for c in range(x.shape[1] // LANES):
        cols.append(_dot_exact_rhs(x[:, c * LANES:(c + 1) * LANES], ones_blk))
    return jnp.concatenate(cols, axis=-1)


def _tri_inverse(a, row, col):
    def same_block(n):
        return (row // n) == (col // n)

    eye = (row == col).astype(F32)
    a16 = jnp.where(same_block(16), a, 0.0)
    t = eye - a16
    apow = a16
    for _ in range(3):
        apow_b = apow.astype(BF16)
        apow = _dot(apow_b, apow_b)
        t = t + _dot(t.astype(BF16), apow.astype(BF16))
    n = 32
    while n <= CHUNK:
        a_off = jnp.where(same_block(n) & jnp.logical_not(same_block(n // 2)), a, 0.0)
        t_b = t.astype(BF16)
        t = t - _dot(t_b, _dot(a_off.astype(BF16), t_b).astype(BF16))
        n *= 2
    return t


def _rwkv_kernel(*refs, reverse, n_lead):
    if reverse:
        (z_ref, zp_ref, zn_ref, yf_ref, bf_ref, sp_ref, sn_ref, w0_ref, w2_ref, a0_ref, a2_ref,
         kk_ref, ka_ref, rk_ref, g2_ref, lnw_ref, lnb_ref, o_ref, st_ref) = refs
    else:
        (z_ref, zp_ref, zn_ref, sp_ref, sn_ref, w0_ref, w2_ref, a0_ref, a2_ref,
         kk_ref, ka_ref, rk_ref, y_ref, bon_ref, st_ref) = refs
    c = pl.program_id(1)
    nc = pl.num_programs(1)
    phys = (nc - c) if reverse else c
    n_phys = nc + 1 if reverse else nc

    @pl.when(c == 0)
    def _():
        st_ref[...] = jnp.zeros_like(st_ref)

    z = z_ref[0]
    rows = lax.broadcasted_iota(jnp.int32, (CHUNK, 1), 0)
    prev_row = jnp.where(phys > 0, zp_ref[0, 7:8, :], 0.0)
    next_row = jnp.where(phys < n_phys - 1, zn_ref[0, 0:1, :], 0.0)
    z_prev = jnp.where(rows == 0, prev_row, pltpu.roll(z, 1, axis=0))
    z_next = jnp.where(rows == CHUNK - 1, next_row, pltpu.roll(z, CHUNK - 1, axis=0))
    zs = z + sp_ref[...] * (z_prev - z) + sn_ref[...] * (z_next - z)

    live = jnp.logical_or(phys > 0, rows >= n_lead).astype(F32)
    w = RWKV_WIDTH
    r = zs[:, :w] * live
    k = zs[:, w:2 * w] * live
    v = zs[:, 2 * w:3 * w] * live
    wd = zs[:, 3 * w:3 * w + LANES]
    ad = zs[:, 3 * w + LANES:3 * w + 2 * LANES]

    lane_r = lax.broadcasted_iota(jnp.int32, (LANES, LANES), 0)
    lane_c = lax.broadcasted_iota(jnp.int32, (LANES, LANES), 1)
    ones_blk = ((lane_r // RWKV_HEAD) == (lane_c // RWKV_HEAD)).astype(BF16)

    w_raw = w0_ref[...] + _dot(jnp.tanh(wd).astype(BF16), w2_ref[...])
    lw = (-math.exp(-0.5)) * jax.nn.sigmoid(w_raw)
    icl = jax.nn.sigmoid(a0_ref[...] + _dot(ad.astype(BF16), a2_ref[...]))
    k_rep = k * (1.0 + (icl - 1.0) * ka_ref[...])
    kk = k * kk_ref[...]
    kk = kk / jnp.maximum(jnp.sqrt(_head_sum(kk * kk, ones_blk)), 1e-12)
    bvec = kk * icl
    bonus = _head_sum(r * k_rep * rk_ref[...], ones_blk) * v

    row = lax.broadcasted_iota(jnp.int32, (CHUNK, CHUNK), 0)
    col = lax.broadcasted_iota(jnp.int32, (CHUNK, CHUNK), 1)
    if reverse:
        incl = col >= row
        strict = col > row
    else:
        incl = col <= row
        strict = col < row
    l_in = _dot_exact_lhs(incl.astype(BF16), lw)
    l_ex = l_in - lw
    l_tot = l_in[0:1, :] if reverse else l_in[CHUNK - 1:CHUNK, :]
    e_neg = jnp.exp(-l_in)
    e_rem = jnp.exp(l_tot - l_in)
    p_tot = jnp.exp(l_tot)
    kd = (kk * jnp.exp(l_ex)).astype(BF16)
    rd = (r * jnp.exp(l_in)).astype(BF16)
    bn = (bvec * e_neg).astype(BF16)
    kn = (k_rep * e_neg).astype(BF16)
    bh = (bvec * e_rem).astype(BF16)
    kh = (k_rep * e_rem).astype(BF16)
    vb = v.astype(BF16)
    p_tot_t = jnp.transpose(jnp.broadcast_to(p_tot, (8, w)))

    ys = []
    for h in range(RWKV_HEADS):
        sl = slice(h * RWKV_HEAD, (h + 1) * RWKV_HEAD)
        kd_h, rd_h, bn_h, kn_h, bh_h, kh_h, v_h = (t[:, sl] for t in (kd, rd, bn, kn, bh, kh, vb))
        a_ab = jnp.where(strict, _dot_nt(kd_h, bn_h), 0.0)
        a_ak = jnp.where(strict, _dot_nt(kd_h, kn_h), 0.0)
        m_rb = jnp.where(incl, _dot_nt(rd_h, bn_h), 0.0).astype(BF16)
        m_rk = jnp.where(incl, _dot_nt(rd_h, kn_h), 0.0).astype(BF16)
        t_inv = _tri_inverse(a_ab, row, col).astype(BF16)
        wmat = -_dot(t_inv, kd_h)
        u0 = -_dot(t_inv, _dot(a_ak.astype(BF16), v_h).astype(BF16))
        wmat_b = wmat.astype(BF16)
        u0_b = u0.astype(BF16)
        st = st_ref[h]
        st_b = st.astype(BF16)
        q_mat = (rd_h.astype(F32) + _dot(m_rb, wmat_b)).astype(BF16)
        ys.append(_dot(q_mat, st_b) + _dot(m_rb, u0_b) + _dot(m_rk, v_h))
        g_mat = _dot_tn(bh_h, wmat_b).astype(BF16)
        h_mat = _dot_tn(bh_h, u0_b) + _dot_tn(kh_h, v_h)
        st_ref[h] = p_tot_t[sl, 0:1] * st + _dot(g_mat, st_b) + h_mat
    y = jnp.concatenate(ys, axis=-1)

    if not reverse:
        y_ref[0] = y
        bon_ref[0] = bonus
    else:
        y = y + yf_ref[0]
        mu = _head_sum(y, ones_blk) * (1.0 / RWKV_HEAD)
        yc = y - mu
        var = _head_sum(yc * yc, ones_blk) * (1.0 / RWKV_HEAD)
        yn = yc * lax.rsqrt(var + GN_EPS) * lnw_ref[...] + lnb_ref[...] + (bonus + bf_ref[0])
        gd = zs[:, 3 * w + 2 * LANES:3 * w + 3 * LANES]
        g = _dot(jax.nn.sigmoid(gd).astype(BF16), g2_ref[...])
        o_ref[0] = (yn * g).astype(BF16)


def _rwkv_direction(zf, reverse, n_lead, shared, dir_params, fwd_out=None, tail=None):
    batch, lp, zc = zf.shape
    n_phys = lp // CHUNK
    nc = n_phys - 1 if reverse else n_phys
    sub = CHUNK // 8
    w = RWKV_WIDTH
    if reverse:
        cmap = lambda b, c: (b, nc - c, 0)
        pmap = lambda b, c: (b, jnp.maximum((nc - c) * sub - 1, 0), 0)
        nmap = lambda b, c: (b, jnp.minimum((nc - c + 1) * sub, n_phys * sub - 1), 0)
    else:
        cmap = lambda b, c: (b, c, 0)
        pmap = lambda b, c: (b, jnp.maximum(c * sub - 1, 0), 0)
        nmap = lambda b, c: (b, jnp.minimum((c + 1) * sub, n_phys * sub - 1), 0)
    full = lambda b, c: (0, 0)
    row_w = pl.BlockSpec((1, w), full)
    in_specs = [
        pl.BlockSpec((1, CHUNK, zc), cmap),
        pl.BlockSpec((1, 8, zc), pmap),
        pl.BlockSpec((1, 8, zc), nmap),
    ]
    args = [zf, zf, zf]
    if reverse:
        in_specs += [pl.BlockSpec((1, CHUNK, w), cmap), pl.BlockSpec((1, CHUNK, w), cmap)]
        args += list(fwd_out)
    in_specs += [pl.BlockSpec((1, zc), full), pl.BlockSpec((1, zc), full),
                 row_w, pl.BlockSpec((LANES, w), full), row_w, pl.BlockSpec((LANES, w), full),
                 row_w, row_w, row_w]
    args += [shared["sp"], shared["sn"], dir_params["w0"], dir_params["w2"], dir_params["a0"],
             dir_params["a2"], shared["k_k"], shared["k_a"], shared["r_k"]]
    if reverse:
        in_specs += [pl.BlockSpec((LANES, w), full), row_w, row_w]
        args += [tail["g2"], tail["ln_w"], tail["ln_b"]]
        out_shape = jax.ShapeDtypeStruct((batch, nc * CHUNK, w), BF16)
        out_specs = pl.BlockSpec((1, CHUNK, w), lambda b, c: (b, nc - 1 - c, 0))
    else:
        out_shape = (jax.ShapeDtypeStruct((batch, lp, w), F32),
                     jax.ShapeDtypeStruct((batch, lp, w), F32))
        out_specs = (pl.BlockSpec((1, CHUNK, w), cmap), pl.BlockSpec((1, CHUNK, w), cmap))
    return pl.pallas_call(
        functools.partial(_rwkv_kernel, reverse=reverse, n_lead=n_lead),
        out_shape=out_shape,
        grid=(batch, nc),
        in_specs=in_specs,
        out_specs=out_specs,
        scratch_shapes=[pltpu.VMEM((RWKV_HEADS, RWKV_HEAD, RWKV_HEAD), F32)],
        compiler_params=_params(("parallel", "arbitrary"), 48),
        name="rwkv7_bwd" if reverse else "rwkv7_fwd",
    )(*args)


def _merge_kernel(oa_ref, ob_ref, ga_ref, gb_ref, wa_ref, wb_ref, o_ref):
    a = _dot(oa_ref[...], wa_ref[...])
    b = _dot(ob_ref[...], wb_ref[...])
    o_ref[...] = (jax.nn.sigmoid(ga_ref[...]) * a + jax.nn.sigmoid(gb_ref[...]) * b).astype(BF16)


def _merge(o_a, o_b, gates, w_a, w_b, tm=512, tn=1024):
    m, d = o_a.shape[0], w_a.shape[1]
    tm = min(tm, m)
    nj = d // tn
    return pl.pallas_call(
        _merge_kernel,
        out_shape=jax.ShapeDtypeStruct((m, d), BF16),
        grid=(m // tm, nj),
        in_specs=[
            pl.BlockSpec((tm, o_a.shape[1]), lambda i, j: (i, 0)),
            pl.BlockSpec((tm, o_b.shape[1]), lambda i, j: (i, 0)),
            pl.BlockSpec((tm, tn), lambda i, j: (i, j)),
            pl.BlockSpec((tm, tn), lambda i, j: (i, j + nj)),
            pl.BlockSpec((w_a.shape[0], tn), lambda i, j: (0, j)),
            pl.BlockSpec((w_b.shape[0], tn), lambda i, j: (0, j)),
        ],
        out_specs=pl.BlockSpec((tm, tn), lambda i, j: (i, j)),
        compiler_params=_params(("parallel", "parallel"), 48),
        name="branch_merge",
    )(o_a, o_b, gates, gates, w_a, w_b)


def _out_router_kernel(x_ref, mg_ref, wo_ref, g_ref, wr_ref, br_ref, h_ref, n2_ref, cmb_ref):
    h = x_ref[...] + _dot(mg_ref[...], wo_ref[...])
    h_ref[...] = h
    n2 = h * lax.rsqrt(jnp.mean(h * h, axis=-1, keepdims=True) + RMS_EPS) * g_ref[...]
    n2_ref[...] = n2.astype(BF16)
    logits = _dot_f32(n2, wr_ref[...]) + br_ref[...]
    lane = lax.broadcasted_iota(jnp.int32, logits.shape, 1)
    big = jnp.int32(1 << 20)
    g_mask = (lane >= N_EXPERTS) & (lane < N_EXPERTS + N_GROUPS)
    gl = jnp.where(g_mask, logits, NEG_BIG)
    g_max = jnp.max(gl, axis=-1, keepdims=True)
    pg_top = 1.0 / jnp.sum(jnp.where(g_mask, jnp.exp(gl - g_max), 0.0), axis=-1, keepdims=True)
    gi = jnp.min(jnp.where(gl == g_max, lane, big), axis=-1, keepdims=True) - N_EXPERTS
    e_mask = (lane >= gi * EXPERTS_PER_GROUP) & (lane < (gi + 1) * EXPERTS_PER_GROUP)
    el = jnp.where(e_mask, logits, NEG_BIG)
    m1 = jnp.max(el, axis=-1, keepdims=True)
    i1 = jnp.min(jnp.where(el == m1, lane, big), axis=-1, keepdims=True)
    el2 = jnp.where(lane == i1, NEG_BIG, el)
    m2 = jnp.max(el2, axis=-1, keepdims=True)
    i2 = jnp.min(jnp.where(el2 == m2, lane, big), axis=-1, keepdims=True)
    e21 = jnp.exp(m2 - m1)
    gate1 = pg_top / (1.0 + e21)
    gate2 = pg_top * e21 / (1.0 + e21)
    cmb_ref[...] = jnp.where(lane == i1, gate1, 0.0) + jnp.where(lane == i2, gate2, 0.0)


def _out_router(x, merged, w_out, gain, w_r, b_r, tm=256):
    m, d = x.shape
    tm = min(tm, m)
    full = lambda i: (0, 0)
    row = lambda i: (i, 0)
    return pl.pallas_call(
        _out_router_kernel,
        out_shape=(jax.ShapeDtypeStruct((m, d), F32), jax.ShapeDtypeStruct((m, d), BF16),
                   jax.ShapeDtypeStruct((m, LANES), F32)),
        grid=(m // tm,),
        in_specs=[
            pl.BlockSpec((tm, d), row), pl.BlockSpec((tm, d), row),
            pl.BlockSpec((d, d), full), pl.BlockSpec((1, d), full),
            pl.BlockSpec((d, LANES), full), pl.BlockSpec((1, LANES), full),
        ],
        out_specs=(pl.BlockSpec((tm, d), row), pl.BlockSpec((tm, d), row),
                   pl.BlockSpec((tm, LANES), row)),
        compiler_params=_params(("parallel",), 48),
        name="out_proj_router",
    )(x, merged, w_out, gain, w_r, b_r)


def _moe_kernel(x_ref, h_ref, cmb_ref, tri_ref, wg_ref, wu_ref, wd_ref, o_ref, pos_ref, cmbt_ref,
                *, rb):
    e = pl.program_id(1)
    tm, d = x_ref.shape

    @pl.when(e == 0)
    def _():
        o_ref[...] = h_ref[...]
        cmb_t = jnp.transpose(cmb_ref[...])
        cmbt_ref[...] = cmb_t
        pos_ref[...] = _dot((cmb_t > 0.0).astype(BF16), tri_ref[...])

    pos = pos_ref[pl.ds(e, 1), :]
    gate = cmbt_ref[pl.ds(e, 1), :]
    routed = gate > 0.0
    count = jnp.max(pos).astype(jnp.int32)
    slot = lax.broadcasted_iota(jnp.int32, (rb, tm), 0)
    pos_i = pos.astype(jnp.int32) - 1
    cb = min(d, 512)

    def body(blk, carry):
        hit = routed & (pos_i == slot + blk * rb)
        onehot = jnp.where(hit, 1.0, 0.0).astype(BF16)
        g_col = jnp.sum(jnp.where(hit, gate, 0.0), axis=-1, keepdims=True)
        xe = _dot(onehot, x_ref[...]).astype(BF16)
        hg = _dot(xe, wg_ref[0])
        hu = _dot(xe, wu_ref[0])
        act = (hg * jax.nn.sigmoid(hg)) * hu * g_col
        ye = _dot(act.astype(BF16), wd_ref[0]).astype(BF16)
        for j in range(d // cb):
            o_ref[:, j * cb:(j + 1) * cb] += _dot_tn(onehot, ye[:, j * cb:(j + 1) * cb])
        return carry

    lax.fori_loop(0, (count + rb - 1) // rb, body, 0)


def _moe(n2, h, cmb, w_gate, w_up, w_down, tm=1024, rb=128):
    m, d = n2.shape
    tm = min(tm, m)
    n_e, _, f = w_gate.shape
    row = lambda i, e: (i, 0)
    once = pl.Buffered(1)
    idx = jnp.arange(tm)
    tri = (idx[:, None] <= idx[None, :]).astype(BF16)
    return pl.pallas_call(
        functools.partial(_moe_kernel, rb=rb),
        out_shape=jax.ShapeDtypeStruct((m, d), F32),
        grid=(m // tm, n_e),
        in_specs=[
            pl.BlockSpec((tm, d), row, pipeline_mode=once),
            pl.BlockSpec((tm, d), row, pipeline_mode=once),
            pl.BlockSpec((tm, LANES), row, pipeline_mode=once),
            pl.BlockSpec((tm, tm), lambda i, e: (0, 0), pipeline_mode=once),
            pl.BlockSpec((1, d, f), lambda i, e: (e, 0, 0)),
            pl.BlockSpec((1, d, f), lambda i, e: (e, 0, 0)),
            pl.BlockSpec((1, f, d), lambda i, e: (e, 0, 0)),
        ],
        out_specs=pl.BlockSpec((tm, d), row),
        scratch_shapes=[pltpu.VMEM((LANES, tm), F32), pltpu.VMEM((LANES, tm), F32)],
        compiler_params=_params(("parallel", "arbitrary"), 56),
        name="moe_experts",
    )(n2, h, cmb, tri, w_gate, w_up, w_down)


def _prepare_weights(p):
    d = p["w_in"].shape[0]
    splits = [Q_LORA, KV_LORA, QK_ROPE, 3 * RWKV_WIDTH + 4 * 64 + 128, d, d]
    offs = [0]
    for s in splits:
        offs.append(offs[-1] + s)
    w_in = p["w_in"].astype(BF16)
    rw = 3 * RWKV_WIDTH
    zeros64 = jnp.zeros((64, RWKV_WIDTH), F32)

    def pad_lo(a):
        return jnp.concatenate([a, zeros64], axis=0).astype(BF16)

    def pad_hi(a):
        return jnp.concatenate([zeros64, a], axis=0).astype(BF16)

    row = lambda a: a.reshape(1, -1).astype(F32)
    wr = jnp.concatenate(
        [jnp.transpose(p["w_router_expert"], (1, 0, 2)).reshape(d, N_EXPERTS), p["w_router_group"],
         jnp.zeros((d, LANES - N_EXPERTS - N_GROUPS), F32)], axis=1)
    br = jnp.concatenate([p["b_router_expert"].reshape(-1), p["b_router_group"],
                          jnp.zeros((LANES - N_EXPERTS - N_GROUPS,), F32)]).reshape(1, LANES)
    return {
        "attn_norm": p["attn_norm"],
        "w_mla": w_in[:, offs[0]:offs[3]],
        "w_rw": w_in[:, offs[3]:offs[4]],
        "w_gates": w_in[:, offs[4]:offs[6]],
        "qa": row(p["q_a_norm"]), "kva": row(p["kv_a_norm"]),
        "wq": p["w_q_b"].astype(BF16), "wkv": p["w_kv_b"].astype(BF16),
        "qn": row(p["q_norm"]), "kn": row(p["k_norm"]),
        "shared": {"sp": row(p["shift_prev"]), "sn": row(p["shift_next"]), "k_k": row(p["k_k"]),
                   "k_a": row(p["k_a"]), "r_k": row(p["r_k"])},
        "fwd": {"w0": row(p["decay_w0_fwd"]), "w2": pad_lo(p["decay_w2_fwd"]),
                "a0": row(p["icl_a0_fwd"]), "a2": pad_lo(p["icl_a2_fwd"])},
        "bwd": {"w0": row(p["decay_w0_bwd"]), "w2": pad_hi(p["decay_w2_bwd"]),
                "a0": row(p["icl_a0_bwd"]), "a2": pad_hi(p["icl_a2_bwd"])},
        "tail": {"g2": p["gate_w2"].astype(BF16), "ln_w": row(p["ln_x_w"]), "ln_b": row(p["ln_x_b"])},
        "w_a": p["w_branch_mla"].astype(BF16), "w_b": p["w_branch_rwkv"].astype(BF16),
        "w_out": p["w_out"].astype(BF16), "ffn_norm": row(p["ffn_norm"]),
        "wr": wr, "br": br,
        "w_gate": p["w_expert_gate"].astype(BF16), "w_up": p["w_expert_up"].astype(BF16),
        "w_down": p["w_expert_down"].astype(BF16),
    }


def _rope_tables(length):
    pos = jnp.arange(length, dtype=F32)
    inv = ROPE_THETA ** (-jnp.arange(0, QK_ROPE, 2, dtype=F32) / QK_ROPE)
    ang = pos[:, None] * inv[None, :]
    cos, sin = jnp.cos(ang), jnp.sin(ang)
    return jnp.concatenate([cos, cos], axis=-1), jnp.concatenate([-sin, sin], axis=-1)


def _trunk(x, meta_tokens, wts):
    batch, seq, d = x.shape
    m = batch * seq
    x2 = x.reshape(m, d)
    meta = meta_tokens.astype(F32)
    cos, sin = _rope_tables(seq + N_META)

    zm_mla = _norm_matmul(meta, wts["attn_norm"], wts["w_mla"], tn=wts["w_mla"].shape[1])
    zm_rw = _norm_matmul(meta, wts["attn_norm"], wts["w_rw"], tn=1152)
    z_mla = _norm_matmul(x2, wts["attn_norm"], wts["w_mla"], tn=wts["w_mla"].shape[1])
    z_rw = _norm_matmul(x2, wts["attn_norm"], wts["w_rw"], tn=1152)
    gates = _norm_matmul(x2, wts["attn_norm"], wts["w_gates"], tn=1024)

    mla_w = (wts["qa"], wts["kva"], wts["wq"], wts["wkv"], wts["qn"], wts["kn"])
    q, k, v = _mla_prep(z_mla, batch, seq, cos[N_META:], sin[N_META:], *mla_w)
    _, km, vm = _mla_prep(zm_mla, 1, N_META, cos[:N_META], sin[:N_META], *mla_w)
    pad = ((0, 0), (0, META_ROWS - N_META), (0, 0))
    o_a = _attention(q, k, v, jnp.pad(km[0], pad), jnp.pad(vm[0], pad))
    o_a = o_a.reshape(m, MLA_HEADS * V_HEAD)

    n_lead = CHUNK - N_META
    lead = jnp.concatenate([jnp.zeros((n_lead, zm_rw.shape[1]), F32), zm_rw], axis=0)
    zf = jnp.concatenate([jnp.broadcast_to(lead[None], (batch,) + lead.shape),
                          z_rw.reshape(batch, seq, -1)], axis=1)
    fwd_out = _rwkv_direction(zf, False, n_lead, wts["shared"], wts["fwd"])
    o_b = _rwkv_direction(zf, True, n_lead, wts["shared"], wts["bwd"], fwd_out=fwd_out,
                          tail=wts["tail"])
    o_b = o_b.reshape(m, RWKV_WIDTH)

    merged = _merge(o_a, o_b, gates, wts["w_a"], wts["w_b"])
    h, n2, cmb = _out_router(x2, merged, wts["w_out"], wts["ffn_norm"], wts["wr"], wts["br"])
    out = _moe(n2, h, cmb, wts["w_gate"], wts["w_up"], wts["w_down"])
    return out.reshape(batch, seq, d)


def kernel(x_prompt, x_sample, meta_tokens, attn_norm, w_in, q_a_norm, w_q_b, kv_a_norm, w_kv_b, q_norm, k_norm, shift_prev, shift_next, decay_w0_fwd, decay_w2_fwd, decay_w0_bwd, decay_w2_bwd, icl_a0_fwd, icl_a2_fwd, icl_a0_bwd, icl_a2_bwd, gate_w2, k_k, k_a, r_k, ln_x_w, ln_x_b, w_branch_mla, w_branch_rwkv, w_out, ffn_norm, w_router_group, b_router_group, w_router_expert, b_router_expert, w_expert_gate, w_expert_up, w_expert_down):
    p = {
        "attn_norm": attn_norm, "w_in": w_in, "q_a_norm": q_a_norm, "w_q_b": w_q_b,
        "kv_a_norm": kv_a_norm, "w_kv_b": w_kv_b, "q_norm": q_norm, "k_norm": k_norm,
        "shift_prev": shift_prev, "shift_next": shift_next,
        "decay_w0_fwd": decay_w0_fwd, "decay_w2_fwd": decay_w2_fwd,
        "decay_w0_bwd": decay_w0_bwd, "decay_w2_bwd": decay_w2_bwd,
        "icl_a0_fwd": icl_a0_fwd, "icl_a2_fwd": icl_a2_fwd,
        "icl_a0_bwd": icl_a0_bwd, "icl_a2_bwd": icl_a2_bwd,
        "gate_w2": gate_w2, "k_k": k_k, "k_a": k_a, "r_k": r_k, "ln_x_w": ln_x_w, "ln_x_b": ln_x_b,
        "w_branch_mla": w_branch_mla, "w_branch_rwkv": w_branch_rwkv, "w_out": w_out,
        "ffn_norm": ffn_norm, "w_router_group": w_router_group, "b_router_group": b_router_group,
        "w_router_expert": w_router_expert, "b_router_expert": b_router_expert,
        "w_expert_gate": w_expert_gate, "w_expert_up": w_expert_up, "w_expert_down": w_expert_down,
    }
    wts = _prepare_weights({name: val[0] for name, val in p.items()})
    return _trunk(x_prompt, meta_tokens, wts), _trunk(x_sample, meta_tokens, wts)
```

```python
import functools
import math

import jax
import jax.numpy as jnp
from jax import lax
from jax.experimental import pallas as pl
from jax.experimental.pallas import tpu as pltpu

F32 = jnp.float32
BF16 = jnp.bfloat16

N_META = 16
RMS_EPS = 1e-6
MLA_HEADS = 8
QK_NOPE = 128
QK_ROPE = 64
QK_HEAD = QK_NOPE + QK_ROPE
V_HEAD = 128
V_WIDE = 256
Q_LORA = 512
KV_LORA = 512
ROPE_THETA = 10000.0
RWKV_HEADS = 16
RWKV_HEAD = 64
RWKV_WIDTH = RWKV_HEADS * RWKV_HEAD
GN_EPS = 64e-5
N_GROUPS = 4
EXPERTS_PER_GROUP = 8
N_EXPERTS = N_GROUPS * EXPERTS_PER_GROUP
D_EXPERT = 512

LANES = 128
CHUNK = 64
META_ROWS = 128
NEG_BIG = -1e30
MIB = 1024 * 1024


def _params(sem, vmem_mib):
    return pltpu.CompilerParams(dimension_semantics=sem, vmem_limit_bytes=vmem_mib * MIB)


def _dot(a, b):
    return jnp.dot(a, b, preferred_element_type=F32)


def _dot_nt(a, b):
    return lax.dot_general(a, b, (((1,), (1,)), ((), ())), preferred_element_type=F32)


def _dot_tn(a, b):
    return lax.dot_general(a, b, (((0,), (0,)), ((), ())), preferred_element_type=F32)


def _split3(x):
    h1 = x.astype(BF16)
    r1 = x - h1.astype(F32)
    h2 = r1.astype(BF16)
    h3 = (r1 - h2.astype(F32)).astype(BF16)
    return h1, h2, h3


def _dot_exact_lhs(a_bf16, x):
    h1, h2, h3 = _split3(x)
    return _dot(a_bf16, h1) + _dot(a_bf16, h2) + _dot(a_bf16, h3)


def _dot_exact_rhs(x, b_bf16):
    h1, h2, h3 = _split3(x)
    return _dot(h1, b_bf16) + _dot(h2, b_bf16) + _dot(h3, b_bf16)


def _dot_f32(x, w):
    x1, x2, x3 = _split3(x)
    w1, w2, w3 = _split3(w)
    return (_dot(x1, w1) + (_dot(x1, w2) + _dot(x2, w1))
            + (_dot(x1, w3) + _dot(x2, w2) + _dot(x3, w1)))


def _norm_matmul_kernel(x_ref, g_ref, w_ref, o_ref, nb_ref):
    @pl.when(pl.program_id(1) == 0)
    def _():
        x = x_ref[...]
        y = x * lax.rsqrt(jnp.mean(x * x, axis=-1, keepdims=True) + RMS_EPS)
        nb_ref[...] = (y * g_ref[...]).astype(BF16)

    o_ref[...] = _dot(nb_ref[...], w_ref[...])


def _norm_matmul(x, gain, w, tn, tm=512):
    m, d = x.shape
    n = w.shape[1]
    tm = min(tm, m)
    assert m % tm == 0 and n % tn == 0
    return pl.pallas_call(
        _norm_matmul_kernel,
        out_shape=jax.ShapeDtypeStruct((m, n), F32),
        grid=(m // tm, n // tn),
        in_specs=[
            pl.BlockSpec((tm, d), lambda i, j: (i, 0)),
            pl.BlockSpec((1, d), lambda i, j: (0, 0)),
            pl.BlockSpec((d, tn), lambda i, j: (0, j)),
        ],
        out_specs=pl.BlockSpec((tm, tn), lambda i, j: (i, j)),
        scratch_shapes=[pltpu.VMEM((tm, d), BF16)],
        compiler_params=_params(("parallel", "arbitrary"), 48),
        name="norm_matmul",
    )(x, gain.reshape(1, d).astype(F32), w)


def _mla_prep_kernel(z_ref, cos_ref, sin_ref, qa_ref, kva_ref, wq_ref, wkv_ref, qn_ref, kn_ref,
                     q_ref, k_ref, v_ref):
    z = z_ref[...]
    cq = z[:, :Q_LORA]
    ckv = z[:, Q_LORA:Q_LORA + KV_LORA]
    kr = z[:, Q_LORA + KV_LORA:]
    cos = cos_ref[...]
    sin = sin_ref[...]

    def rms(x, g):
        return x * lax.rsqrt(jnp.mean(x * x, axis=-1, keepdims=True) + RMS_EPS) * g

    def rope(x):
        swapped = jnp.concatenate([x[:, QK_ROPE // 2:], x[:, :QK_ROPE // 2]], axis=-1)
        return x * cos + swapped * sin

    q_all = _dot(rms(cq, qa_ref[...]).astype(BF16), wq_ref[...])
    kv_all = _dot(rms(ckv, kva_ref[...]).astype(BF16), wkv_ref[...])
    qn = qn_ref[...]
    kn = kn_ref[...]
    scale = QK_HEAD ** -0.5 * math.log2(math.e)
    ones_col = (lax.broadcasted_iota(jnp.int32, (z.shape[0], V_WIDE - V_HEAD), 1) == 0).astype(BF16)
    kr_sq = jnp.sum(kr * kr, axis=-1, keepdims=True)
    kr_rot = rope(kr * kn[:, QK_NOPE:])
    for h in range(MLA_HEADS):
        qh = q_all[:, h * QK_HEAD:(h + 1) * QK_HEAD]
        qh = qh * (lax.rsqrt(jnp.mean(qh * qh, axis=-1, keepdims=True) + RMS_EPS) * scale) * qn
        q_ref[0, h, :, :QK_NOPE] = qh[:, :QK_NOPE].astype(BF16)
        q_ref[0, h, :, QK_NOPE:] = rope(qh[:, QK_NOPE:]).astype(BF16)
        base = h * (QK_NOPE + V_HEAD)
        k_nope = kv_all[:, base:base + QK_NOPE]
        ms = (jnp.sum(k_nope * k_nope, axis=-1, keepdims=True) + kr_sq) * (1.0 / QK_HEAD)
        inv = lax.rsqrt(ms + RMS_EPS)
        k_ref[0, h, :, :QK_NOPE] = (k_nope * inv * kn[:, :QK_NOPE]).astype(BF16)
        k_ref[0, h, :, QK_NOPE:] = (kr_rot * inv).astype(BF16)
        v_ref[0, h, :, :V_HEAD] = kv_all[:, base + QK_NOPE:base + QK_NOPE + V_HEAD].astype(BF16)
        v_ref[0, h, :, V_HEAD:] = ones_col


def _mla_prep(z_mla, batch, seq, cos, sin, qa, kva, wq, wkv, qn, kn, tm=512):
    tm = min(tm, seq)
    assert seq % tm == 0
    nt = seq // tm
    zc = z_mla.shape[1]
    full = lambda b, i: (0, 0)
    return pl.pallas_call(
        _mla_prep_kernel,
        out_shape=(
            jax.ShapeDtypeStruct((batch, MLA_HEADS, seq, QK_HEAD), BF16),
            jax.ShapeDtypeStruct((batch, MLA_HEADS, seq, QK_HEAD), BF16),
            jax.ShapeDtypeStruct((batch, MLA_HEADS, seq, V_WIDE), BF16),
        ),
        grid=(batch, nt),
        in_specs=[
            pl.BlockSpec((tm, zc), lambda b, i: (b * nt + i, 0)),
            pl.BlockSpec((tm, QK_ROPE), lambda b, i: (i, 0)),
            pl.BlockSpec((tm, QK_ROPE), lambda b, i: (i, 0)),
            pl.BlockSpec((1, Q_LORA), full),
            pl.BlockSpec((1, KV_LORA), full),
            pl.BlockSpec(wq.shape, full),
            pl.BlockSpec(wkv.shape, full),
            pl.BlockSpec((1, QK_HEAD), full),
            pl.BlockSpec((1, QK_HEAD), full),
        ],
        out_specs=(
            pl.BlockSpec((1, MLA_HEADS, tm, QK_HEAD), lambda b, i: (b, 0, i, 0)),
            pl.BlockSpec((1, MLA_HEADS, tm, QK_HEAD), lambda b, i: (b, 0, i, 0)),
            pl.BlockSpec((1, MLA_HEADS, tm, V_WIDE), lambda b, i: (b, 0, i, 0)),
        ),
        compiler_params=_params(("parallel", "parallel"), 48),
        name="mla_prep",
    )(z_mla, cos, sin, qa, kva, wq, wkv, qn, kn)


def _attention_kernel(q_ref, k_ref, v_ref, km_ref, vm_ref, o_ref, *, tk, n_split):
    q = q_ref[0, 0]
    tq = q.shape[0]
    seq = k_ref.shape[2]

    rows = tq // n_split
    qs = [q[g * rows:(g + 1) * rows] for g in range(n_split)]
    ss = [_dot_nt(x, km_ref[0]) for x in qs]
    lane = lax.broadcasted_iota(jnp.int32, ss[0].shape, 1)
    ss = [jnp.where(lane < N_META, s, NEG_BIG) for s in ss]
    ms = [jnp.max(s, axis=-1, keepdims=True) for s in ss]
    accs = [_dot(jnp.exp2(s - m).astype(BF16), vm_ref[0]) for s, m in zip(ss, ms)]

    def body(c, carry):
        ms, accs = carry
        off = pl.multiple_of(c * tk, tk)
        k = k_ref[0, 0, pl.ds(off, tk), :]
        v = v_ref[0, 0, pl.ds(off, tk), :]
        ss = [_dot_nt(x, k) for x in qs]
        m_new = [jnp.maximum(m, jnp.max(s, axis=-1, keepdims=True)) for m, s in zip(ms, ss)]
        ps = [jnp.exp2(s - m).astype(BF16) for s, m in zip(ss, m_new)]
        accs = [jnp.exp2(m - mn) * a + _dot(p, v) for m, mn, a, p in zip(ms, m_new, accs, ps)]
        return m_new, accs

    ms, accs = lax.fori_loop(0, seq // tk, body, (ms, accs))
    for g in range(n_split):
        acc = accs[g]
        o_ref[0, g * rows:(g + 1) * rows, :] = (
            acc[:, :V_HEAD] / acc[:, V_HEAD:V_HEAD + 1]).astype(BF16)


def _attention(q, k, v, k_meta, v_meta, tq=1024, tk=2048, n_split=4):
    batch, heads, seq, _ = q.shape
    tq = min(tq, seq)
    tk = min(tk, seq)
    assert seq % tq == 0 and seq % tk == 0 and tq % (16 * n_split) == 0
    return pl.pallas_call(
        functools.partial(_attention_kernel, tk=tk, n_split=n_split),
        out_shape=jax.ShapeDtypeStruct((batch, seq, heads * V_HEAD), BF16),
        grid=(batch, heads, seq // tq),
        in_specs=[
            pl.BlockSpec((1, 1, tq, QK_HEAD), lambda b, h, i: (b, h, i, 0)),
            pl.BlockSpec((1, 1, seq, QK_HEAD), lambda b, h, i: (b, h, 0, 0)),
            pl.BlockSpec((1, 1, seq, V_WIDE), lambda b, h, i: (b, h, 0, 0)),
            pl.BlockSpec((1, META_ROWS, QK_HEAD), lambda b, h, i: (h, 0, 0)),
            pl.BlockSpec((1, META_ROWS, V_WIDE), lambda b, h, i: (h, 0, 0)),
        ],
        out_specs=pl.BlockSpec((1, tq, V_HEAD), lambda b, h, i: (b, i, h)),
        compiler_params=_params(("parallel", "parallel", "arbitrary"), 48),
        name="mla_attention",
    )(q, k, v, k_meta, v_meta)


def _head_sum(x, ones_blk):
    cols = []
    for c in range(x.shape[1] // LANES):
        cols.append(_dot_exact_rhs(x[:, c * LANES:(c + 1) * LANES], ones_blk))
    return jnp.concatenate(cols, axis=-1)


def _tri_inverse(a_list, row, col):
    def same_block(n):
        return (row // n) == (col // n)

    eye = (row == col).astype(F32)
    apow = [jnp.where(same_block(16), a, 0.0) for a in a_list]
    t = [eye - a for a in apow]
    for _ in range(3):
        apow_b = [a.astype(BF16) for a in apow]
        apow = [_dot(a, a) for a in apow_b]
        t = [ti + _dot(ti.astype(BF16), a.astype(BF16)) for ti, a in zip(t, apow)]
    n = 32
    while n <= CHUNK:
        off_mask = same_block(n) & jnp.logical_not(same_block(n // 2))
        t_b = [ti.astype(BF16) for ti in t]
        inner = [_dot(jnp.where(off_mask, a, 0.0).astype(BF16), tb).astype(BF16)
                 for a, tb in zip(a_list, t_b)]
        t = [ti - _dot(tb, x) for ti, tb, x in zip(t, t_b, inner)]
        n *= 2
    return t


def _rwkv_kernel(*refs, reverse, n_lead):
    if reverse:
        (z_ref, zp_ref, zn_ref, yf_ref, bf_ref, sp_ref, sn_ref, w0_ref, w2_ref, a0_ref, a2_ref,
         kk_ref, ka_ref, rk_ref, g2_ref, lnw_ref, lnb_ref, o_ref, st_ref) = refs
    else:
        (z_ref, zp_ref, zn_ref, sp_ref, sn_ref, w0_ref, w2_ref, a0_ref, a2_ref,
         kk_ref, ka_ref, rk_ref, y_ref, bon_ref, st_ref) = refs
    c = pl.program_id(1)
    nc = pl.num_programs(1)
    phys = (nc - c) if reverse else c
    n_phys = nc + 1 if reverse else nc

    @pl.when(c == 0)
    def _():
        st_ref[...] = jnp.zeros_like(st_ref)

    z = z_ref[0]
    rows = lax.broadcasted_iota(jnp.int32, (CHUNK, 1), 0)
    prev_row = jnp.where(phys > 0, zp_ref[0, 7:8, :], 0.0)
    next_row = jnp.where(phys < n_phys - 1, zn_ref[0, 0:1, :], 0.0)
    z_prev = jnp.where(rows == 0, prev_row, pltpu.roll(z, 1, axis=0))
    z_next = jnp.where(rows == CHUNK - 1, next_row, pltpu.roll(z, CHUNK - 1, axis=0))
    zs = z + sp_ref[...] * (z_prev - z) + sn_ref[...] * (z_next - z)

    live = jnp.logical_or(phys > 0, rows >= n_lead).astype(F32)
    w = RWKV_WIDTH
    r = zs[:, :w] * live
    k = zs[:, w:2 * w] * live
    v = zs[:, 2 * w:3 * w] * live
    wd = zs[:, 3 * w:3 * w + LANES]
    ad = zs[:, 3 * w + LANES:3 * w + 2 * LANES]

    lane_r = lax.broadcasted_iota(jnp.int32, (LANES, LANES), 0)
    lane_c = lax.broadcasted_iota(jnp.int32, (LANES, LANES), 1)
    ones_blk = ((lane_r // RWKV_HEAD) == (lane_c // RWKV_HEAD)).astype(BF16)

    w_raw = w0_ref[...] + _dot(jnp.tanh(wd).astype(BF16), w2_ref[...])
    lw = (-math.exp(-0.5)) * jax.nn.sigmoid(w_raw)
    icl = jax.nn.sigmoid(a0_ref[...] + _dot(ad.astype(BF16), a2_ref[...]))
    k_rep = k * (1.0 + (icl - 1.0) * ka_ref[...])
    kk = k * kk_ref[...]
    kk = kk / jnp.maximum(jnp.sqrt(_head_sum(kk * kk, ones_blk)), 1e-12)
    bvec = kk * icl
    bonus = _head_sum(r * k_rep * rk_ref[...], ones_blk) * v

    row = lax.broadcasted_iota(jnp.int32, (CHUNK, CHUNK), 0)
    col = lax.broadcasted_iota(jnp.int32, (CHUNK, CHUNK), 1)
    if reverse:
        incl = col >= row
        strict = col > row
    else:
        incl = col <= row
        strict = col < row
    l_in = _dot_exact_lhs(incl.astype(BF16), lw)
    l_ex = l_in - lw
    l_tot = l_in[0:1, :] if reverse else l_in[CHUNK - 1:CHUNK, :]
    e_neg = jnp.exp(-l_in)
    e_rem = jnp.exp(l_tot - l_in)
    p_tot = jnp.exp(l_tot)
    kd = (kk * jnp.exp(l_ex)).astype(BF16)
    rd = (r * jnp.exp(l_in)).astype(BF16)
    bn = (bvec * e_neg).astype(BF16)
    kn = (k_rep * e_neg).astype(BF16)
    bh = (bvec * e_rem).astype(BF16)
    kh = (k_rep * e_rem).astype(BF16)
    vb = v.astype(BF16)
    p_tot_t = jnp.transpose(jnp.broadcast_to(p_tot, (8, w)))

    heads = range(RWKV_HEADS)
    sls = [slice(h * RWKV_HEAD, (h + 1) * RWKV_HEAD) for h in heads]
    kd_h, rd_h, bn_h, kn_h, bh_h, kh_h, v_h = ([t[:, sl] for sl in sls]
                                               for t in (kd, rd, bn, kn, bh, kh, vb))
    kr_h = [jnp.concatenate([a, b], axis=0) for a, b in zip(kd_h, rd_h)]
    prod_b = [_dot_nt(x, y) for x, y in zip(kr_h, bn_h)]
    prod_k = [_dot_nt(x, y) for x, y in zip(kr_h, kn_h)]
    a_ab = [jnp.where(strict, x[:CHUNK], 0.0) for x in prod_b]
    m_rb = [jnp.where(incl, x[CHUNK:], 0.0).astype(BF16) for x in prod_b]
    a_ak = [jnp.where(strict, x[:CHUNK], 0.0).astype(BF16) for x in prod_k]
    m_rk = [jnp.where(incl, x[CHUNK:], 0.0).astype(BF16) for x in prod_k]
    akv = [_dot(a, vv).astype(BF16) for a, vv in zip(a_ak, v_h)]
    t_inv = [t.astype(BF16) for t in _tri_inverse(a_ab, row, col)]
    wmat = [(-_dot(t, x)).astype(BF16) for t, x in zip(t_inv, kd_h)]
    u0 = [(-_dot(t, x)).astype(BF16) for t, x in zip(t_inv, akv)]
    st = [st_ref[h] for h in heads]
    st_b = [s.astype(BF16) for s in st]
    q_mat = [(x.astype(F32) + _dot(m, wm)).astype(BF16) for x, m, wm in zip(rd_h, m_rb, wmat)]
    ys = [_dot(qm, sb) + _dot(mb, u) + _dot(mk, vv)
          for qm, sb, mb, u, mk, vv in zip(q_mat, st_b, m_rb, u0, m_rk, v_h)]
    g_mat = [_dot_tn(b, wm).astype(BF16) for b, wm in zip(bh_h, wmat)]
    h_mat = [_dot_tn(b, u) + _dot_tn(kx, vv) for b, u, kx, vv in zip(bh_h, u0, kh_h, v_h)]
    for h in heads:
        st_ref[h] = p_tot_t[sls[h], 0:1] * st[h] + _dot(g_mat[h], st_b[h]) + h_mat[h]
    y = jnp.concatenate(ys, axis=-1)

    if not reverse:
        y_ref[0] = y
        bon_ref[0] = bonus
    else:
        y = y + yf_ref[0]
        mu = _head_sum(y, ones_blk) * (1.0 / RWKV_HEAD)
        yc = y - mu
        var = _head_sum(yc * yc, ones_blk) * (1.0 / RWKV_HEAD)
        yn = yc * lax.rsqrt(var + GN_EPS) * lnw_ref[...] + lnb_ref[...] + (bonus + bf_ref[0])
        gd = zs[:, 3 * w + 2 * LANES:3 * w + 3 * LANES]
        g = _dot(jax.nn.sigmoid(gd).astype(BF16), g2_ref[...])
        o_ref[0] = (yn * g).astype(BF16)


def _rwkv_direction(zf, reverse, n_lead, shared, dir_params, fwd_out=None, tail=None):
    batch, lp, zc = zf.shape
    n_phys = lp // CHUNK
    nc = n_phys - 1 if reverse else n_phys
    sub = CHUNK // 8
    w = RWKV_WIDTH
    if reverse:
        cmap = lambda b, c: (b, nc - c, 0)
        pmap = lambda b, c: (b, jnp.maximum((nc - c) * sub - 1, 0), 0)
        nmap = lambda b, c: (b, jnp.minimum((nc - c + 1) * sub, n_phys * sub - 1), 0)
    else:
        cmap = lambda b, c: (b, c, 0)
        pmap = lambda b, c: (b, jnp.maximum(c * sub - 1, 0), 0)
        nmap = lambda b, c: (b, jnp.minimum((c + 1) * sub, n_phys * sub - 1), 0)
    full = lambda b, c: (0, 0)
    row_w = pl.BlockSpec((1, w), full)
    in_specs = [
        pl.BlockSpec((1, CHUNK, zc), cmap),
        pl.BlockSpec((1, 8, zc), pmap),
        pl.BlockSpec((1, 8, zc), nmap),
    ]
    args = [zf, zf, zf]
    if reverse:
        in_specs += [pl.BlockSpec((1, CHUNK, w), cmap), pl.BlockSpec((1, CHUNK, w), cmap)]
        args += list(fwd_out)
    in_specs += [pl.BlockSpec((1, zc), full), pl.BlockSpec((1, zc), full),
                 row_w, pl.BlockSpec((LANES, w), full), row_w, pl.BlockSpec((LANES, w), full),
                 row_w, row_w, row_w]
    args += [shared["sp"], shared["sn"], dir_params["w0"], dir_params["w2"], dir_params["a0"],
             dir_params["a2"], shared["k_k"], shared["k_a"], shared["r_k"]]
    if reverse:
        in_specs += [pl.BlockSpec((LANES, w), full), row_w, row_w]
        args += [tail["g2"], tail["ln_w"], tail["ln_b"]]
        out_shape = jax.ShapeDtypeStruct((batch, nc * CHUNK, w), BF16)
        out_specs = pl.BlockSpec((1, CHUNK, w), lambda b, c: (b, nc - 1 - c, 0))
    else:
        out_shape = (jax.ShapeDtypeStruct((batch, lp, w), F32),
                     jax.ShapeDtypeStruct((batch, lp, w), F32))
        out_specs = (pl.BlockSpec((1, CHUNK, w), cmap), pl.BlockSpec((1, CHUNK, w), cmap))
    return pl.pallas_call(
        functools.partial(_rwkv_kernel, reverse=reverse, n_lead=n_lead),
        out_shape=out_shape,
        grid=(batch, nc),
        in_specs=in_specs,
        out_specs=out_specs,
        scratch_shapes=[pltpu.VMEM((RWKV_HEADS, RWKV_HEAD, RWKV_HEAD), F32)],
        compiler_params=_params(("parallel", "arbitrary"), 48),
        name="rwkv7_bwd" if reverse else "rwkv7_fwd",
    )(*args)


def _merge_kernel(oa_ref, ob_ref, ga_ref, gb_ref, wa_ref, wb_ref, o_ref):
    a = _dot(oa_ref[...], wa_ref[...])
    b = _dot(ob_ref[...], wb_ref[...])
    o_ref[...] = (jax.nn.sigmoid(ga_ref[...]) * a + jax.nn.sigmoid(gb_ref[...]) * b).astype(BF16)


def _merge(o_a, o_b, gates, w_a, w_b, tm=512, tn=1024):
    m, d = o_a.shape[0], w_a.shape[1]
    tm = min(tm, m)
    nj = d // tn
    return pl.pallas_call(
        _merge_kernel,
        out_shape=jax.ShapeDtypeStruct((m, d), BF16),
        grid=(m // tm, nj),
        in_specs=[
            pl.BlockSpec((tm, o_a.shape[1]), lambda i, j: (i, 0)),
            pl.BlockSpec((tm, o_b.shape[1]), lambda i, j: (i, 0)),
            pl.BlockSpec((tm, tn), lambda i, j: (i, j)),
            pl.BlockSpec((tm, tn), lambda i, j: (i, j + nj)),
            pl.BlockSpec((w_a.shape[0], tn), lambda i, j: (0, j)),
            pl.BlockSpec((w_b.shape[0], tn), lambda i, j: (0, j)),
        ],
        out_specs=pl.BlockSpec((tm, tn), lambda i, j: (i, j)),
        compiler_params=_params(("parallel", "parallel"), 48),
        name="branch_merge",
    )(o_a, o_b, gates, gates, w_a, w_b)


def _out_router_kernel(x_ref, mg_ref, wo_ref, g_ref, wr_ref, br_ref, h_ref, n2_ref, cmb_ref):
    h = x_ref[...] + _dot(mg_ref[...], wo_ref[...])
    h_ref[...] = h
    n2 = h * lax.rsqrt(jnp.mean(h * h, axis=-1, keepdims=True) + RMS_EPS) * g_ref[...]
    n2_ref[...] = n2.astype(BF16)
    logits = _dot_f32(n2, wr_ref[...]) + br_ref[...]
    lane = lax.broadcasted_iota(jnp.int32, logits.shape, 1)
    big = jnp.int32(1 << 20)
    g_mask = (lane >= N_EXPERTS) & (lane < N_EXPERTS + N_GROUPS)
    gl = jnp.where(g_mask, logits, NEG_BIG)
    g_max = jnp.max(gl, axis=-1, keepdims=True)
    pg_top = 1.0 / jnp.sum(jnp.where(g_mask, jnp.exp(gl - g_max), 0.0), axis=-1, keepdims=True)
    gi = jnp.min(jnp.where(gl == g_max, lane, big), axis=-1, keepdims=True) - N_EXPERTS
    e_mask = (lane >= gi * EXPERTS_PER_GROUP) & (lane < (gi + 1) * EXPERTS_PER_GROUP)
    el = jnp.where(e_mask, logits, NEG_BIG)
    m1 = jnp.max(el, axis=-1, keepdims=True)
    i1 = jnp.min(jnp.where(el == m1, lane, big), axis=-1, keepdims=True)
    el2 = jnp.where(lane == i1, NEG_BIG, el)
    m2 = jnp.max(el2, axis=-1, keepdims=True)
    i2 = jnp.min(jnp.where(el2 == m2, lane, big), axis=-1, keepdims=True)
    e21 = jnp.exp(m2 - m1)
    gate1 = pg_top / (1.0 + e21)
    gate2 = pg_top * e21 / (1.0 + e21)
    cmb_ref[...] = jnp.where(lane == i1, gate1, 0.0) + jnp.where(lane == i2, gate2, 0.0)


def _out_router(x, merged, w_out, gain, w_r, b_r, tm=256):
    m, d = x.shape
    tm = min(tm, m)
    full = lambda i: (0, 0)
    row = lambda i: (i, 0)
    return pl.pallas_call(
        _out_router_kernel,
        out_shape=(jax.ShapeDtypeStruct((m, d), F32), jax.ShapeDtypeStruct((m, d), BF16),
                   jax.ShapeDtypeStruct((m, LANES), F32)),
        grid=(m // tm,),
        in_specs=[
            pl.BlockSpec((tm, d), row), pl.BlockSpec((tm, d), row),
            pl.BlockSpec((d, d), full), pl.BlockSpec((1, d), full),
            pl.BlockSpec((d, LANES), full), pl.BlockSpec((1, LANES), full),
        ],
        out_specs=(pl.BlockSpec((tm, d), row), pl.BlockSpec((tm, d), row),
                   pl.BlockSpec((tm, LANES), row)),
        compiler_params=_params(("parallel",), 48),
        name="out_proj_router",
    )(x, merged, w_out, gain, w_r, b_r)


def _moe_kernel(x_ref, h_ref, cmb_ref, tri_ref, wg_ref, wu_ref, wd_ref, o_ref, pos_ref, cmbt_ref,
                *, rb):
    e = pl.program_id(1)
    tm, d = x_ref.shape

    @pl.when(e == 0)
    def _():
        o_ref[...] = h_ref[...]
        cmb_t = jnp.transpose(cmb_ref[...])
        cmbt_ref[...] = cmb_t
        pos_ref[...] = _dot((cmb_t > 0.0).astype(BF16), tri_ref[...])

    pos = pos_ref[pl.ds(e, 1), :]
    gate = cmbt_ref[pl.ds(e, 1), :]
    routed = gate > 0.0
    count = jnp.max(pos).astype(jnp.int32)
    slot = lax.broadcasted_iota(jnp.int32, (rb, tm), 0)
    pos_i = pos.astype(jnp.int32) - 1
    cb = min(d, 512)

    def body(blk, carry):
        hit = routed & (pos_i == slot + blk * rb)
        onehot = jnp.where(hit, 1.0, 0.0).astype(BF16)
        g_col = jnp.sum(jnp.where(hit, gate, 0.0), axis=-1, keepdims=True)
        xe = _dot(onehot, x_ref[...]).astype(BF16)
        hg = _dot(xe, wg_ref[0])
        hu = _dot(xe, wu_ref[0])
        act = (hg * jax.nn.sigmoid(hg)) * hu * g_col
        ye = _dot(act.astype(BF16), wd_ref[0]).astype(BF16)
        for j in range(d // cb):
            o_ref[:, j * cb:(j + 1) * cb] += _dot_tn(onehot, ye[:, j * cb:(j + 1) * cb])
        return carry

    lax.fori_loop(0, (count + rb - 1) // rb, body, 0)


def _moe(n2, h, cmb, w_gate, w_up, w_down, tm=1024, rb=128):
    m, d = n2.shape
    tm = min(tm, m)
    n_e, _, f = w_gate.shape
    row = lambda i, e: (i, 0)
    once = pl.Buffered(1)
    idx = jnp.arange(tm)
    tri = (idx[:, None] <= idx[None, :]).astype(BF16)
    return pl.pallas_call(
        functools.partial(_moe_kernel, rb=rb),
        out_shape=jax.ShapeDtypeStruct((m, d), F32),
        grid=(m // tm, n_e),
        in_specs=[
            pl.BlockSpec((tm, d), row, pipeline_mode=once),
            pl.BlockSpec((tm, d), row, pipeline_mode=once),
            pl.BlockSpec((tm, LANES), row, pipeline_mode=once),
            pl.BlockSpec((tm, tm), lambda i, e: (0, 0), pipeline_mode=once),
            pl.BlockSpec((1, d, f), lambda i, e: (e, 0, 0)),
            pl.BlockSpec((1, d, f), lambda i, e: (e, 0, 0)),
            pl.BlockSpec((1, f, d), lambda i, e: (e, 0, 0)),
        ],
        out_specs=pl.BlockSpec((tm, d), row),
        scratch_shapes=[pltpu.VMEM((LANES, tm), F32), pltpu.VMEM((LANES, tm), F32)],
        compiler_params=_params(("parallel", "arbitrary"), 56),
        name="moe_experts",
    )(n2, h, cmb, tri, w_gate, w_up, w_down)


def _prepare_weights(p):
    d = p["w_in"].shape[0]
    splits = [Q_LORA, KV_LORA, QK_ROPE, 3 * RWKV_WIDTH + 4 * 64 + 128, d, d]
    offs = [0]
    for s in splits:
        offs.append(offs[-1] + s)
    w_in = p["w_in"].astype(BF16)
    rw = 3 * RWKV_WIDTH
    zeros64 = jnp.zeros((64, RWKV_WIDTH), F32)

    def pad_lo(a):
        return jnp.concatenate([a, zeros64], axis=0).astype(BF16)

    def pad_hi(a):
        return jnp.concatenate([zeros64, a], axis=0).astype(BF16)

    row = lambda a: a.reshape(1, -1).astype(F32)
    wr = jnp.concatenate(
        [jnp.transpose(p["w_router_expert"], (1, 0, 2)).reshape(d, N_EXPERTS), p["w_router_group"],
         jnp.zeros((d, LANES - N_EXPERTS - N_GROUPS), F32)], axis=1)
    br = jnp.concatenate([p["b_router_expert"].reshape(-1), p["b_router_group"],
                          jnp.zeros((LANES - N_EXPERTS - N_GROUPS,), F32)]).reshape(1, LANES)
    return {
        "attn_norm": p["attn_norm"],
        "w_mla": w_in[:, offs[0]:offs[3]],
        "w_rw": w_in[:, offs[3]:offs[4]],
        "w_gates": w_in[:, offs[4]:offs[6]],
        "qa": row(p["q_a_norm"]), "kva": row(p["kv_a_norm"]),
        "wq": p["w_q_b"].astype(BF16), "wkv": p["w_kv_b"].astype(BF16),
        "qn": row(p["q_norm"]), "kn": row(p["k_norm"]),
        "shared": {"sp": row(p["shift_prev"]), "sn": row(p["shift_next"]), "k_k": row(p["k_k"]),
                   "k_a": row(p["k_a"]), "r_k": row(p["r_k"])},
        "fwd": {"w0": row(p["decay_w0_fwd"]), "w2": pad_lo(p["decay_w2_fwd"]),
                "a0": row(p["icl_a0_fwd"]), "a2": pad_lo(p["icl_a2_fwd"])},
        "bwd": {"w0": row(p["decay_w0_bwd"]), "w2": pad_hi(p["decay_w2_bwd"]),
                "a0": row(p["icl_a0_bwd"]), "a2": pad_hi(p["icl_a2_bwd"])},
        "tail": {"g2": p["gate_w2"].astype(BF16), "ln_w": row(p["ln_x_w"]), "ln_b": row(p["ln_x_b"])},
        "w_a": p["w_branch_mla"].astype(BF16), "w_b": p["w_branch_rwkv"].astype(BF16),
        "w_out": p["w_out"].astype(BF16), "ffn_norm": row(p["ffn_norm"]),
        "wr": wr, "br": br,
        "w_gate": p["w_expert_gate"].astype(BF16), "w_up": p["w_expert_up"].astype(BF16),
        "w_down": p["w_expert_down"].astype(BF16),
    }


def _rope_tables(length):
    pos = jnp.arange(length, dtype=F32)
    inv = ROPE_THETA ** (-jnp.arange(0, QK_ROPE, 2, dtype=F32) / QK_ROPE)
    ang = pos[:, None] * inv[None, :]
    cos, sin = jnp.cos(ang), jnp.sin(ang)
    return jnp.concatenate([cos, cos], axis=-1), jnp.concatenate([-sin, sin], axis=-1)


def _trunk(x, meta_tokens, wts):
    batch, seq, d = x.shape
    m = batch * seq
    x2 = x.reshape(m, d)
    meta = meta_tokens.astype(F32)
    cos, sin = _rope_tables(seq + N_META)

    zm_mla = _norm_matmul(meta, wts["attn_norm"], wts["w_mla"], tn=wts["w_mla"].shape[1])
    zm_rw = _norm_matmul(meta, wts["attn_norm"], wts["w_rw"], tn=1152)
    z_mla = _norm_matmul(x2, wts["attn_norm"], wts["w_mla"], tn=wts["w_mla"].shape[1])
    z_rw = _norm_matmul(x2, wts["attn_norm"], wts["w_rw"], tn=1152)
    gates = _norm_matmul(x2, wts["attn_norm"], wts["w_gates"], tn=1024)

    mla_w = (wts["qa"], wts["kva"], wts["wq"], wts["wkv"], wts["qn"], wts["kn"])
    q, k, v = _mla_prep(z_mla, batch, seq, cos[N_META:], sin[N_META:], *mla_w)
    _, km, vm = _mla_prep(zm_mla, 1, N_META, cos[:N_META], sin[:N_META], *mla_w)
    pad = ((0, 0), (0, META_ROWS - N_META), (0, 0))
    o_a = _attention(q, k, v, jnp.pad(km[0], pad), jnp.pad(vm[0], pad))
    o_a = o_a.reshape(m, MLA_HEADS * V_HEAD)

    n_lead = CHUNK - N_META
    lead = jnp.concatenate([jnp.zeros((n_lead, zm_rw.shape[1]), F32), zm_rw], axis=0)
    zf = jnp.concatenate([jnp.broadcast_to(lead[None], (batch,) + lead.shape),
                          z_rw.reshape(batch, seq, -1)], axis=1)
    fwd_out = _rwkv_direction(zf, False, n_lead, wts["shared"], wts["fwd"])
    o_b = _rwkv_direction(zf, True, n_lead, wts["shared"], wts["bwd"], fwd_out=fwd_out,
                          tail=wts["tail"])
    o_b = o_b.reshape(m, RWKV_WIDTH)

    merged = _merge(o_a, o_b, gates, wts["w_a"], wts["w_b"])
    h, n2, cmb = _out_router(x2, merged, wts["w_out"], wts["ffn_norm"], wts["wr"], wts["br"])
    out = _moe(n2, h, cmb, wts["w_gate"], wts["w_up"], wts["w_down"])
    return out.reshape(batch, seq, d)


def kernel(x_prompt, x_sample, meta_tokens, attn_norm, w_in, q_a_norm, w_q_b, kv_a_norm, w_kv_b, q_norm, k_norm, shift_prev, shift_next, decay_w0_fwd, decay_w2_fwd, decay_w0_bwd, decay_w2_bwd, icl_a0_fwd, icl_a2_fwd, icl_a0_bwd, icl_a2_bwd, gate_w2, k_k, k_a, r_k, ln_x_w, ln_x_b, w_branch_mla, w_branch_rwkv, w_out, ffn_norm, w_router_group, b_router_group, w_router_expert, b_router_expert, w_expert_gate, w_expert_up, w_expert_down):
    p = {
        "attn_norm": attn_norm, "w_in": w_in, "q_a_norm": q_a_norm, "w_q_b": w_q_b,
        "kv_a_norm": kv_a_norm, "w_kv_b": w_kv_b, "q_norm": q_norm, "k_norm": k_norm,
        "shift_prev": shift_prev, "shift_next": shift_next,
        "decay_w0_fwd": decay_w0_fwd, "decay_w2_fwd": decay_w2_fwd,
        "decay_w0_bwd": decay_w0_bwd, "decay_w2_bwd": decay_w2_bwd,
        "icl_a0_fwd": icl_a0_fwd, "icl_a2_fwd": icl_a2_fwd,
        "icl_a0_bwd": icl_a0_bwd, "icl_a2_bwd": icl_a2_bwd,
        "gate_w2": gate_w2, "k_k": k_k, "k_a": k_a, "r_k": r_k, "ln_x_w": ln_x_w, "ln_x_b": ln_x_b,
        "w_branch_mla": w_branch_mla, "w_branch_rwkv": w_branch_rwkv, "w_out": w_out,
        "ffn_norm": ffn_norm, "w_router_group": w_router_group, "b_router_group": b_router_group,
        "w_router_expert": w_router_expert, "b_router_expert": b_router_expert,
        "w_expert_gate": w_expert_gate, "w_expert_up": w_expert_up, "w_expert_down": w_expert_down,
    }
    wts = _prepare_weights({name: val[0] for name, val in p.items()})
    return _trunk(x_prompt, meta_tokens, wts), _trunk(x_sample, meta_tokens, wts)
```

```python
import functools
import math

import jax
import jax.numpy as jnp
from jax import lax
from jax.experimental import pallas as pl
from jax.experimental.pallas import tpu as pltpu

F32 = jnp.float32
BF16 = jnp.bfloat16

N_META = 16
RMS_EPS = 1e-6
MLA_HEADS = 8
QK_NOPE = 128
QK_ROPE = 64
QK_HEAD = QK_NOPE + QK_ROPE
V_HEAD = 128
V_WIDE = 256
Q_LORA = 512
KV_LORA = 512
ROPE_THETA = 10000.0
RWKV_HEADS = 16
RWKV_HEAD = 64
RWKV_WIDTH = RWKV_HEADS * RWKV_HEAD
GN_EPS = 64e-5
N_GROUPS = 4
EXPERTS_PER_GROUP = 8
N_EXPERTS = N_GROUPS * EXPERTS_PER_GROUP
D_EXPERT = 512

LANES = 128
CHUNK = 64
META_ROWS = 128
NEG_BIG = -1e30
MIB = 1024 * 1024


def _params(sem, vmem_mib):
    return pltpu.CompilerParams(dimension_semantics=sem, vmem_limit_bytes=vmem_mib * MIB)


def _dot(a, b):
    return jnp.dot(a, b, preferred_element_type=F32)


def _dot_nt(a, b):
    return lax.dot_general(a, b, (((1,), (1,)), ((), ())), preferred_element_type=F32)


def _dot_tn(a, b):
    return lax.dot_general(a, b, (((0,), (0,)), ((), ())), preferred_element_type=F32)


def _split2(x):
    hi = x.astype(BF16)
    lo = (x - hi.astype(F32)).astype(BF16)
    return hi, lo


def _dot_exact_lhs(a_bf16, x):
    hi, lo = _split2(x)
    return _dot(a_bf16, hi) + _dot(a_bf16, lo)


def _dot_wide(x, w):
    x1, x2 = _split2(x)
    w1, w2 = _split2(w)
    return _dot(x1, w1) + (_dot(x1, w2) + _dot(x2, w1))


def _norm_matmul_kernel(x_ref, g_ref, w_ref, o_ref, nb_ref):
    @pl.when(pl.program_id(1) == 0)
    def _():
        x = x_ref[...]
        y = x * lax.rsqrt(jnp.mean(x * x, axis=-1, keepdims=True) + RMS_EPS)
        nb_ref[...] = (y * g_ref[...]).astype(BF16)

    o_ref[...] = _dot(nb_ref[...], w_ref[...])


def _norm_matmul(x, gain, w, tn, tm=512):
    m, d = x.shape
    n = w.shape[1]
    tm = min(tm, m)
    assert m % tm == 0 and n % tn == 0
    return pl.pallas_call(
        _norm_matmul_kernel,
        out_shape=jax.ShapeDtypeStruct((m, n), F32),
        grid=(m // tm, n // tn),
        in_specs=[
            pl.BlockSpec((tm, d), lambda i, j: (i, 0)),
            pl.BlockSpec((1, d), lambda i, j: (0, 0)),
            pl.BlockSpec((d, tn), lambda i, j: (0, j)),
        ],
        out_specs=pl.BlockSpec((tm, tn), lambda i, j: (i, j)),
        scratch_shapes=[pltpu.VMEM((tm, d), BF16)],
        compiler_params=_params(("parallel", "arbitrary"), 48),
        name="norm_matmul",
    )(x, gain.reshape(1, d).astype(F32), w)


def _mla_prep_kernel(z_ref, cos_ref, sin_ref, qa_ref, kva_ref, wq_ref, wkv_ref, qn_ref, kn_ref,
                     q_ref, k_ref, v_ref):
    z = z_ref[...]
    cq = z[:, :Q_LORA]
    ckv = z[:, Q_LORA:Q_LORA + KV_LORA]
    kr = z[:, Q_LORA + KV_LORA:]
    cos = cos_ref[...]
    sin = sin_ref[...]

    def rms(x, g):
        return x * lax.rsqrt(jnp.mean(x * x, axis=-1, keepdims=True) + RMS_EPS) * g

    def rope(x):
        swapped = jnp.concatenate([x[:, QK_ROPE // 2:], x[:, :QK_ROPE // 2]], axis=-1)
        return x * cos + swapped * sin

    q_all = _dot(rms(cq, qa_ref[...]).astype(BF16), wq_ref[...])
    kv_all = _dot(rms(ckv, kva_ref[...]).astype(BF16), wkv_ref[...])
    qn = qn_ref[...]
    kn = kn_ref[...]
    scale = QK_HEAD ** -0.5 * math.log2(math.e)
    ones_col = (lax.broadcasted_iota(jnp.int32, (z.shape[0], V_WIDE - V_HEAD), 1) == 0).astype(BF16)
    kr_sq = jnp.sum(kr * kr, axis=-1, keepdims=True)
    kr_rot = rope(kr * kn[:, QK_NOPE:])
    for h in range(MLA_HEADS):
        qh = q_all[:, h * QK_HEAD:(h + 1) * QK_HEAD]
        qh = qh * (lax.rsqrt(jnp.mean(qh * qh, axis=-1, keepdims=True) + RMS_EPS) * scale) * qn
        q_ref[0, h, :, :QK_NOPE] = qh[:, :QK_NOPE].astype(BF16)
        q_ref[0, h, :, QK_NOPE:] = rope(qh[:, QK_NOPE:]).astype(BF16)
        base = h * (QK_NOPE + V_HEAD)
        k_nope = kv_all[:, base:base + QK_NOPE]
        ms = (jnp.sum(k_nope * k_nope, axis=-1, keepdims=True) + kr_sq) * (1.0 / QK_HEAD)
        inv = lax.rsqrt(ms + RMS_EPS)
        k_ref[0, h, :, :QK_NOPE] = (k_nope * inv * kn[:, :QK_NOPE]).astype(BF16)
        k_ref[0, h, :, QK_NOPE:] = (kr_rot * inv).astype(BF16)
        v_ref[0, h, :, :V_HEAD] = kv_all[:, base + QK_NOPE:base + QK_NOPE + V_HEAD].astype(BF16)
        v_ref[0, h, :, V_HEAD:] = ones_col


def _mla_prep(z_mla, batch, seq, cos, sin, qa, kva, wq, wkv, qn, kn, tm=512):
    tm = min(tm, seq)
    assert seq % tm == 0
    nt = seq // tm
    zc = z_mla.shape[1]
    full = lambda b, i: (0, 0)
    return pl.pallas_call(
        _mla_prep_kernel,
        out_shape=(
            jax.ShapeDtypeStruct((batch, MLA_HEADS, seq, QK_HEAD), BF16),
            jax.ShapeDtypeStruct((batch, MLA_HEADS, seq, QK_HEAD), BF16),
            jax.ShapeDtypeStruct((batch, MLA_HEADS, seq, V_WIDE), BF16),
        ),
        grid=(batch, nt),
        in_specs=[
            pl.BlockSpec((tm, zc), lambda b, i: (b * nt + i, 0)),
            pl.BlockSpec((tm, QK_ROPE), lambda b, i: (i, 0)),
            pl.BlockSpec((tm, QK_ROPE), lambda b, i: (i, 0)),
            pl.BlockSpec((1, Q_LORA), full),
            pl.BlockSpec((1, KV_LORA), full),
            pl.BlockSpec(wq.shape, full),
            pl.BlockSpec(wkv.shape, full),
            pl.BlockSpec((1, QK_HEAD), full),
            pl.BlockSpec((1, QK_HEAD), full),
        ],
        out_specs=(
            pl.BlockSpec((1, MLA_HEADS, tm, QK_HEAD), lambda b, i: (b, 0, i, 0)),
            pl.BlockSpec((1, MLA_HEADS, tm, QK_HEAD), lambda b, i: (b, 0, i, 0)),
            pl.BlockSpec((1, MLA_HEADS, tm, V_WIDE), lambda b, i: (b, 0, i, 0)),
        ),
        compiler_params=_params(("parallel", "parallel"), 48),
        name="mla_prep",
    )(z_mla, cos, sin, qa, kva, wq, wkv, qn, kn)


def _attention_kernel(q_ref, k_ref, v_ref, km_ref, vm_ref, o_ref, *, tk, n_split):
    q = q_ref[0, 0]
    tq = q.shape[0]
    seq = k_ref.shape[2]

    rows = tq // n_split
    qs = [q[g * rows:(g + 1) * rows] for g in range(n_split)]
    ss = [_dot_nt(x, km_ref[0]) for x in qs]
    lane = lax.broadcasted_iota(jnp.int32, ss[0].shape, 1)
    ss = [jnp.where(lane < N_META, s, NEG_BIG) for s in ss]
    ms = [jnp.max(s, axis=-1, keepdims=True) for s in ss]
    accs = [_dot(jnp.exp2(s - m).astype(BF16), vm_ref[0]) for s, m in zip(ss, ms)]

    def body(c, carry):
        ms, accs = carry
        off = pl.multiple_of(c * tk, tk)
        k = k_ref[0, 0, pl.ds(off, tk), :]
        v = v_ref[0, 0, pl.ds(off, tk), :]
        ss = [_dot_nt(x, k) for x in qs]
        m_new = [jnp.maximum(m, jnp.max(s, axis=-1, keepdims=True)) for m, s in zip(ms, ss)]
        ps = [jnp.exp2(s - m).astype(BF16) for s, m in zip(ss, m_new)]
        accs = [jnp.exp2(m - mn) * a + _dot(p, v) for m, mn, a, p in zip(ms, m_new, accs, ps)]
        return m_new, accs

    ms, accs = lax.fori_loop(0, seq // tk, body, (ms, accs))
    for g in range(n_split):
        acc = accs[g]
        o_ref[0, g * rows:(g + 1) * rows, :] = (
            acc[:, :V_HEAD] / acc[:, V_HEAD:V_HEAD + 1]).astype(BF16)


def _attention(q, k, v, k_meta, v_meta, tq=1024, tk=2048, n_split=4):
    batch, heads, seq, _ = q.shape
    tq = min(tq, seq)
    tk = min(tk, seq)
    assert seq % tq == 0 and seq % tk == 0 and tq % (16 * n_split) == 0
    return pl.pallas_call(
        functools.partial(_attention_kernel, tk=tk, n_split=n_split),
        out_shape=jax.ShapeDtypeStruct((batch, seq, heads * V_HEAD), BF16),
        grid=(batch, heads, seq // tq),
        in_specs=[
            pl.BlockSpec((1, 1, tq, QK_HEAD), lambda b, h, i: (b, h, i, 0)),
            pl.BlockSpec((1, 1, seq, QK_HEAD), lambda b, h, i: (b, h, 0, 0)),
            pl.BlockSpec((1, 1, seq, V_WIDE), lambda b, h, i: (b, h, 0, 0)),
            pl.BlockSpec((1, META_ROWS, QK_HEAD), lambda b, h, i: (h, 0, 0)),
            pl.BlockSpec((1, META_ROWS, V_WIDE), lambda b, h, i: (h, 0, 0)),
        ],
        out_specs=pl.BlockSpec((1, tq, V_HEAD), lambda b, h, i: (b, i, h)),
        compiler_params=_params(("parallel", "parallel", "arbitrary"), 48),
        name="mla_attention",
    )(q, k, v, k_meta, v_meta)


def _head_sum(x, ones_blk):
    xb = x.astype(BF16)
    cols = [_dot(xb[:, c * LANES:(c + 1) * LANES], ones_blk) for c in range(x.shape[1] // LANES)]
    return jnp.concatenate(cols, axis=-1)


def _block_diag(x, bd_mask):
    return jnp.where(bd_mask, jnp.concatenate([x, x], axis=0), jnp.zeros((), x.dtype))


def _tri_inverse(a_list, row, col, bd_mask):
    def same_block(n):
        return (row // n) == (col // n)

    eye = (row == col).astype(F32)
    apow = [jnp.where(same_block(16), a, 0.0) for a in a_list]
    t = [eye - a for a in apow]
    for _ in range(3):
        apow_b = [a.astype(BF16) for a in apow]
        apow = [_dot(a, _block_diag(a, bd_mask)) for a in apow_b]
        t = [ti + _dot(ti.astype(BF16), _block_diag(a.astype(BF16), bd_mask))
             for ti, a in zip(t, apow)]
    n = 32
    while n <= CHUNK:
        off_mask = same_block(n) & jnp.logical_not(same_block(n // 2))
        t_bd = [_block_diag(ti.astype(BF16), bd_mask) for ti in t]
        inner = [_dot(jnp.where(off_mask, a, 0.0).astype(BF16), tb).astype(BF16)
                 for a, tb in zip(a_list, t_bd)]
        t = [ti - _dot(ti.astype(BF16), _block_diag(x, bd_mask)) for ti, x in zip(t, inner)]
        n *= 2
    return t


def _rwkv_kernel(*refs, reverse, n_lead, nb):
    if reverse:
        (zs_ref, kkn_ref, yf_ref, bf_ref, w0_ref, w2_ref, a0_ref, a2_ref, ka_ref, rk_ref,
         g2_ref, lnw_ref, lnb_ref, o_ref, st_ref) = refs
    else:
        (z_ref, zp_ref, zn_ref, lead_ref, sp_ref, sn_ref, w0_ref, w2_ref, a0_ref, a2_ref,
         kk_ref, ka_ref, rk_ref, y_ref, bon_ref, zs_ref, kkn_ref, st_ref) = refs
    c = pl.program_id(1)
    nc = pl.num_programs(1)
    phys = (nc - c) if reverse else c

    @pl.when(c == 0)
    def _():
        st_ref[...] = jnp.zeros_like(st_ref)

    w = RWKV_WIDTH
    rows = lax.broadcasted_iota(jnp.int32, (CHUNK, 1), 0)
    lane_r = lax.broadcasted_iota(jnp.int32, (LANES, LANES), 0)
    lane_c = lax.broadcasted_iota(jnp.int32, (LANES, LANES), 1)
    bd_mask = (lane_r // RWKV_HEAD) == (lane_c // RWKV_HEAD)
    ones_blk = bd_mask.astype(BF16)
    row = lax.broadcasted_iota(jnp.int32, (CHUNK, LANES), 0)
    lane = lax.broadcasted_iota(jnp.int32, (CHUNK, LANES), 1)
    col = lane % CHUNK
    first_head = lane < RWKV_HEAD
    if reverse:
        incl = col >= row
        strict = col > row
    else:
        incl = col <= row
        strict = col < row
    tri = incl[:, :CHUNK].astype(BF16)

    def operands(i):
        if reverse:
            zs = zs_ref[i]
            kkn = kkn_ref[i]
            r, k, v = zs[:, :w], zs[:, w:2 * w], zs[:, 2 * w:3 * w]
        else:
            z = jnp.where(phys == 0, lead_ref[...], z_ref[i])
            prev_row = jnp.where(phys == 1, lead_ref[CHUNK - 1:CHUNK, :], zp_ref[i, 7:8, :])
            prev_row = jnp.where(phys > 0, prev_row, 0.0)
            next_row = jnp.where(phys < nc - 1, zn_ref[i, 0:1, :], 0.0)
            z_prev = jnp.where(rows == 0, prev_row, pltpu.roll(z, 1, axis=0))
            z_next = jnp.where(rows == CHUNK - 1, next_row, pltpu.roll(z, CHUNK - 1, axis=0))
            zs = z + sp_ref[...] * (z_prev - z) + sn_ref[...] * (z_next - z)
            zs_ref[i] = zs
            live = jnp.logical_or(phys > 0, rows >= n_lead).astype(F32)
            r = zs[:, :w] * live
            k = zs[:, w:2 * w] * live
            v = zs[:, 2 * w:3 * w] * live
            kk = k * kk_ref[...]
            kkn = kk * jnp.minimum(lax.rsqrt(_head_sum(kk * kk, ones_blk)), 1e12)
            kkn_ref[i] = kkn
        wd = zs[:, 3 * w:3 * w + LANES]
        ad = zs[:, 3 * w + LANES:3 * w + 2 * LANES]
        w_raw = w0_ref[...] + _dot(jnp.tanh(wd).astype(BF16), w2_ref[...])
        lw = (-math.exp(-0.5)) * jax.nn.sigmoid(w_raw)
        icl = jax.nn.sigmoid(a0_ref[...] + _dot(ad.astype(BF16), a2_ref[...]))
        k_rep = k * (1.0 + (icl - 1.0) * ka_ref[...])
        bvec = kkn * icl
        bonus = _head_sum(r * k_rep * rk_ref[...], ones_blk) * v
        l_in = _dot_exact_lhs(tri, lw)
        l_tot = l_in[0:1, :] if reverse else l_in[CHUNK - 1:CHUNK, :]
        e_neg = jnp.exp(-l_in)
        e_rem = jnp.exp(l_tot - l_in)
        ops = {
            "kd": (kkn * jnp.exp(l_in - lw)).astype(BF16), "rd": (r * jnp.exp(l_in)).astype(BF16),
            "bn": (bvec * e_neg).astype(BF16), "kn": (k_rep * e_neg).astype(BF16),
            "bh": (bvec * e_rem).astype(BF16), "kh": (k_rep * e_rem).astype(BF16),
            "v": v.astype(BF16),
        }
        p_tot_t = jnp.transpose(jnp.broadcast_to(jnp.exp(l_tot), (8, w)))
        return ops, p_tot_t, bonus, zs

    per_seq = [operands(i) for i in range(nb)]

    n_pairs = RWKV_HEADS // 2
    items = [(i, p) for i in range(nb) for p in range(n_pairs)]
    bd = lambda x: _block_diag(x, bd_mask)
    diag_blocks = lambda x: jnp.where(first_head, x[:RWKV_HEAD], x[RWKV_HEAD:])
    kd_p, rd_p, bn_p, kn_p, bh_p, kh_p, v_p = (
        [per_seq[i][0][name][:, p * LANES:(p + 1) * LANES] for i, p in items]
        for name in ("kd", "rd", "bn", "kn", "bh", "kh", "v"))
    v_bd = [bd(x) for x in v_p]
    kr_p = [jnp.concatenate([a, b], axis=0) for a, b in zip(kd_p, rd_p)]
    prod_b = [_dot_nt(x, bd(y)) for x, y in zip(kr_p, bn_p)]
    prod_k = [_dot_nt(x, bd(y)) for x, y in zip(kr_p, kn_p)]
    a_ab = [jnp.where(strict, x[:CHUNK], 0.0) for x in prod_b]
    m_rb = [jnp.where(incl, x[CHUNK:], 0.0).astype(BF16) for x in prod_b]
    a_ak = [jnp.where(strict, x[:CHUNK], 0.0).astype(BF16) for x in prod_k]
    m_rk = [jnp.where(incl, x[CHUNK:], 0.0).astype(BF16) for x in prod_k]
    akv = [_dot(a, vv).astype(BF16) for a, vv in zip(a_ak, v_bd)]
    t_inv = [t.astype(BF16) for t in _tri_inverse(a_ab, row, col, bd_mask)]
    wmat = [(-_dot(t, bd(x))).astype(BF16) for t, x in zip(t_inv, kd_p)]
    u0 = [(-_dot(t, bd(x))).astype(BF16) for t, x in zip(t_inv, akv)]
    u0_bd = [bd(x) for x in u0]
    st = [st_ref[i * n_pairs + p] for i, p in items]
    st_bd = [bd(s.astype(BF16)) for s in st]
    q_mat = [(x.astype(F32) + _dot(m, bd(wm))).astype(BF16)
             for x, m, wm in zip(rd_p, m_rb, wmat)]
    ys = [_dot(qm, sb) + _dot(mb, u) + _dot(mk, vv)
          for qm, sb, mb, u, mk, vv in zip(q_mat, st_bd, m_rb, u0_bd, m_rk, v_bd)]
    g_mat = [diag_blocks(_dot_tn(b, wm)).astype(BF16) for b, wm in zip(bh_p, wmat)]
    h_mat = [diag_blocks(_dot_tn(b, u) + _dot_tn(kx, vv))
             for b, u, kx, vv in zip(bh_p, u0, kh_p, v_p)]
    for n, (i, p) in enumerate(items):
        p_tot_t = per_seq[i][1]
        decay = jnp.where(first_head, p_tot_t[2 * p * RWKV_HEAD:(2 * p + 1) * RWKV_HEAD, 0:1],
                          p_tot_t[(2 * p + 1) * RWKV_HEAD:(2 * p + 2) * RWKV_HEAD, 0:1])
        st_ref[i * n_pairs + p] = decay * st[n] + _dot(g_mat[n], st_bd[n]) + h_mat[n]

    for i in range(nb):
        y = jnp.concatenate(ys[i * n_pairs:(i + 1) * n_pairs], axis=-1)
        _, _, bonus, zs = per_seq[i]
        if not reverse:
            y_ref[i] = y
            bon_ref[i] = bonus
        else:
            y = y + yf_ref[i]
            mu = _head_sum(y, ones_blk) * (1.0 / RWKV_HEAD)
            yc = y - mu
            var = _head_sum(yc * yc, ones_blk) * (1.0 / RWKV_HEAD)
            yn = (yc * lax.rsqrt(var + GN_EPS) * lnw_ref[...] + lnb_ref[...]
                  + (bonus + bf_ref[i]))
            gd = zs[:, 3 * w + 2 * LANES:3 * w + 3 * LANES]
            g = _dot(jax.nn.sigmoid(gd).astype(BF16), g2_ref[...])
            o_ref[i] = (yn * g).astype(BF16)


def _rwkv(z_rw, lead, shared, fwd_p, bwd_p, tail, nb=2):
    batch, seq, zc = z_rw.shape
    nb = min(nb, batch)
    assert batch % nb == 0 and seq % CHUNK == 0
    ncx = seq // CHUNK
    lp = seq + CHUNK
    sub = CHUNK // 8
    w = RWKV_WIDTH
    n_lead = CHUNK - N_META
    full = lambda b, c: (0, 0)
    row_w = pl.BlockSpec((1, w), full)
    mat_w = pl.BlockSpec((LANES, w), full)
    state = pltpu.VMEM((nb * RWKV_HEADS // 2, RWKV_HEAD, 2 * RWKV_HEAD), F32)

    chunk = lambda b, c: (b, c, 0)
    fwd_specs = [
        pl.BlockSpec((nb, CHUNK, zc), lambda b, c: (b, jnp.maximum(c - 1, 0), 0)),
        pl.BlockSpec((nb, 8, zc), lambda b, c: (b, jnp.maximum((c - 1) * sub - 1, 0), 0)),
        pl.BlockSpec((nb, 8, zc), lambda b, c: (b, jnp.minimum(c * sub, ncx * sub - 1), 0)),
        pl.BlockSpec((CHUNK, zc), full),
        pl.BlockSpec((1, zc), full), pl.BlockSpec((1, zc), full),
        row_w, mat_w, row_w, mat_w, row_w, row_w, row_w,
    ]
    fwd_args = [z_rw, z_rw, z_rw, lead, shared["sp"], shared["sn"], fwd_p["w0"], fwd_p["w2"],
                fwd_p["a0"], fwd_p["a2"], shared["k_k"], shared["k_a"], shared["r_k"]]
    y_f, bon_f, zs, kkn = pl.pallas_call(
        functools.partial(_rwkv_kernel, reverse=False, n_lead=n_lead, nb=nb),
        out_shape=(jax.ShapeDtypeStruct((batch, lp, w), F32),
                   jax.ShapeDtypeStruct((batch, lp, w), F32),
                   jax.ShapeDtypeStruct((batch, lp, zc), F32),
                   jax.ShapeDtypeStruct((batch, lp, w), F32)),
        grid=(batch // nb, ncx + 1),
        in_specs=fwd_specs,
        out_specs=(pl.BlockSpec((nb, CHUNK, w), chunk), pl.BlockSpec((nb, CHUNK, w), chunk),
                   pl.BlockSpec((nb, CHUNK, zc), chunk), pl.BlockSpec((nb, CHUNK, w), chunk)),
        scratch_shapes=[state],
        compiler_params=_params(("parallel", "arbitrary"), 48),
        name="rwkv7_fwd",
    )(*fwd_args)

    rchunk = lambda b, c: (b, ncx - c, 0)
    bwd_specs = [
        pl.BlockSpec((nb, CHUNK, zc), rchunk), pl.BlockSpec((nb, CHUNK, w), rchunk),
        pl.BlockSpec((nb, CHUNK, w), rchunk), pl.BlockSpec((nb, CHUNK, w), rchunk),
        row_w, mat_w, row_w, mat_w, row_w, row_w, mat_w, row_w, row_w,
    ]
    bwd_args = [zs, kkn, y_f, bon_f, bwd_p["w0"], bwd_p["w2"], bwd_p["a0"], bwd_p["a2"],
                shared["k_a"], shared["r_k"], tail["g2"], tail["ln_w"], tail["ln_b"]]
    return pl.pallas_call(
        functools.partial(_rwkv_kernel, reverse=True, n_lead=n_lead, nb=nb),
        out_shape=jax.ShapeDtypeStruct((batch, seq, w), BF16),
        grid=(batch // nb, ncx),
        in_specs=bwd_specs,
        out_specs=pl.BlockSpec((nb, CHUNK, w), lambda b, c: (b, ncx - 1 - c, 0)),
        scratch_shapes=[state],
        compiler_params=_params(("parallel", "arbitrary"), 48),
        name="rwkv7_bwd",
    )(*bwd_args)


def _merge_kernel(oa_ref, ob_ref, ga_ref, gb_ref, wa_ref, wb_ref, o_ref):
    a = _dot(oa_ref[...], wa_ref[...])
    b = _dot(ob_ref[...], wb_ref[...])
    o_ref[...] = (jax.nn.sigmoid(ga_ref[...]) * a + jax.nn.sigmoid(gb_ref[...]) * b).astype(BF16)


def _merge(o_a, o_b, gates, w_a, w_b, tm=512, tn=1024):
    m, d = o_a.shape[0], w_a.shape[1]
    tm = min(tm, m)
    nj = d // tn
    return pl.pallas_call(
        _merge_kernel,
        out_shape=jax.ShapeDtypeStruct((m, d), BF16),
        grid=(m // tm, nj),
        in_specs=[
            pl.BlockSpec((tm, o_a.shape[1]), lambda i, j: (i, 0)),
            pl.BlockSpec((tm, o_b.shape[1]), lambda i, j: (i, 0)),
            pl.BlockSpec((tm, tn), lambda i, j: (i, j)),
            pl.BlockSpec((tm, tn), lambda i, j: (i, j + nj)),
            pl.BlockSpec((w_a.shape[0], tn), lambda i, j: (0, j)),
            pl.BlockSpec((w_b.shape[0], tn), lambda i, j: (0, j)),
        ],
        out_specs=pl.BlockSpec((tm, tn), lambda i, j: (i, j)),
        compiler_params=_params(("parallel", "parallel"), 48),
        name="branch_merge",
    )(o_a, o_b, gates, gates, w_a, w_b)


def _out_router_kernel(x_ref, mg_ref, wo_ref, g_ref, wr_ref, br_ref, h_ref, n2_ref, cmb_ref):
    h = x_ref[...] + _dot(mg_ref[...], wo_ref[...])
    h_ref[...] = h
    n2 = h * lax.rsqrt(jnp.mean(h * h, axis=-1, keepdims=True) + RMS_EPS) * g_ref[...]
    n2_ref[...] = n2.astype(BF16)
    logits = _dot_wide(n2, wr_ref[...]) + br_ref[...]
    lane = lax.broadcasted_iota(jnp.int32, logits.shape, 1)
    big = jnp.int32(1 << 20)
    g_mask = (lane >= N_EXPERTS) & (lane < N_EXPERTS + N_GROUPS)
    gl = jnp.where(g_mask, logits, NEG_BIG)
    g_max = jnp.max(gl, axis=-1, keepdims=True)
    pg_top = 1.0 / jnp.sum(jnp.where(g_mask, jnp.exp(gl - g_max), 0.0), axis=-1, keepdims=True)
    gi = jnp.min(jnp.where(gl == g_max, lane, big), axis=-1, keepdims=True) - N_EXPERTS
    e_mask = (lane >= gi * EXPERTS_PER_GROUP) & (lane < (gi + 1) * EXPERTS_PER_GROUP)
    el = jnp.where(e_mask, logits, NEG_BIG)
    m1 = jnp.max(el, axis=-1, keepdims=True)
    i1 = jnp.min(jnp.where(el == m1, lane, big), axis=-1, keepdims=True)
    el2 = jnp.where(lane == i1, NEG_BIG, el)
    m2 = jnp.max(el2, axis=-1, keepdims=True)
    i2 = jnp.min(jnp.where(el2 == m2, lane, big), axis=-1, keepdims=True)
    e21 = jnp.exp(m2 - m1)
    gate1 = pg_top / (1.0 + e21)
    gate2 = pg_top * e21 / (1.0 + e21)
    cmb_ref[...] = jnp.where(lane == i1, gate1, 0.0) + jnp.where(lane == i2, gate2, 0.0)


def _out_router(x, merged, w_out, gain, w_r, b_r, tm=256):
    m, d = x.shape
    tm = min(tm, m)
    full = lambda i: (0, 0)
    row = lambda i: (i, 0)
    return pl.pallas_call(
        _out_router_kernel,
        out_shape=(jax.ShapeDtypeStruct((m, d), F32), jax.ShapeDtypeStruct((m, d), BF16),
                   jax.ShapeDtypeStruct((m, LANES), F32)),
        grid=(m // tm,),
        in_specs=[
            pl.BlockSpec((tm, d), row), pl.BlockSpec((tm, d), row),
            pl.BlockSpec((d, d), full), pl.BlockSpec((1, d), full),
            pl.BlockSpec((d, LANES), full), pl.BlockSpec((1, LANES), full),
        ],
        out_specs=(pl.BlockSpec((tm, d), row), pl.BlockSpec((tm, d), row),
                   pl.BlockSpec((tm, LANES), row)),
        compiler_params=_params(("parallel",), 48),
        name="out_proj_router",
    )(x, merged, w_out, gain, w_r, b_r)


def _moe_kernel(x_ref, h_ref, cmb_ref, tri_ref, wg_ref, wu_ref, wd_ref, o_ref, pos_ref, cmbt_ref,
                hot_ref, ye_ref, cnt_ref, *, rb):
    e = pl.program_id(1)
    tm, d = x_ref.shape
    cb = min(d, 512)

    @pl.when(e == 0)
    def _():
        o_ref[...] = h_ref[...]
        cmb_t = jnp.transpose(cmb_ref[...])
        cmbt_ref[...] = cmb_t
        pos_ref[...] = _dot((cmb_t > 0.0).astype(BF16), tri_ref[...])
        hot_ref[...] = jnp.zeros_like(hot_ref)
        ye_ref[...] = jnp.zeros_like(ye_ref)
        cnt_ref[0] = 0

    def flush():
        hot = hot_ref[...]
        for j in range(d // cb):
            o_ref[:, j * cb:(j + 1) * cb] += _dot_tn(hot, ye_ref[:, j * cb:(j + 1) * cb])

    pos = pos_ref[pl.ds(e, 1), :]
    gate = cmbt_ref[pl.ds(e, 1), :]
    routed = gate > 0.0
    count = jnp.max(pos).astype(jnp.int32)
    slot = lax.broadcasted_iota(jnp.int32, (rb, tm), 0)
    pos_i = pos.astype(jnp.int32) - 1

    def body(blk, carry):
        hit = routed & (pos_i == slot + blk * rb)
        onehot = jnp.where(hit, 1.0, 0.0).astype(BF16)
        g_col = jnp.sum(jnp.where(hit, gate, 0.0), axis=-1, keepdims=True)
        xe = _dot(onehot, x_ref[...]).astype(BF16)
        hg = _dot(xe, wg_ref[0])
        hu = _dot(xe, wu_ref[0])
        act = (hg * jax.nn.sigmoid(hg)) * hu * g_col
        half = cnt_ref[0] % 2
        off = pl.multiple_of(half * rb, rb)
        hot_ref[pl.ds(off, rb), :] = onehot
        ye_ref[pl.ds(off, rb), :] = _dot(act.astype(BF16), wd_ref[0]).astype(BF16)
        cnt_ref[0] = cnt_ref[0] + 1
        pl.when(half == 1)(flush)
        return carry

    lax.fori_loop(0, (count + rb - 1) // rb, body, 0)

    @pl.when((e == pl.num_programs(1) - 1) & (cnt_ref[0] % 2 == 1))
    def _():
        hot_ref[rb:, :] = jnp.zeros((rb, tm), BF16)
        flush()


def _moe(n2, h, cmb, w_gate, w_up, w_down, tm=1024, rb=128):
    m, d = n2.shape
    tm = min(tm, m)
    n_e, _, f = w_gate.shape
    row = lambda i, e: (i, 0)
    once = pl.Buffered(1)
    idx = jnp.arange(tm)
    tri = (idx[:, None] <= idx[None, :]).astype(BF16)
    return pl.pallas_call(
        functools.partial(_moe_kernel, rb=rb),
        out_shape=jax.ShapeDtypeStruct((m, d), F32),
        grid=(m // tm, n_e),
        in_specs=[
            pl.BlockSpec((tm, d), row, pipeline_mode=once),
            pl.BlockSpec((tm, d), row, pipeline_mode=once),
            pl.BlockSpec((tm, LANES), row, pipeline_mode=once),
            pl.BlockSpec((tm, tm), lambda i, e: (0, 0), pipeline_mode=once),
            pl.BlockSpec((1, d, f), lambda i, e: (e, 0, 0)),
            pl.BlockSpec((1, d, f), lambda i, e: (e, 0, 0)),
            pl.BlockSpec((1, f, d), lambda i, e: (e, 0, 0)),
        ],
        out_specs=pl.BlockSpec((tm, d), row),
        scratch_shapes=[pltpu.VMEM((LANES, tm), F32), pltpu.VMEM((LANES, tm), F32),
                        pltpu.VMEM((2 * rb, tm), BF16), pltpu.VMEM((2 * rb, d), BF16),
                        pltpu.SMEM((1,), jnp.int32)],
        compiler_params=_params(("parallel", "arbitrary"), 56),
        name="moe_experts",
    )(n2, h, cmb, tri, w_gate, w_up, w_down)


def _prepare_weights(p):
    d = p["w_in"].shape[0]
    splits = [Q_LORA, KV_LORA, QK_ROPE, 3 * RWKV_WIDTH + 4 * 64 + 128, d, d]
    offs = [0]
    for s in splits:
        offs.append(offs[-1] + s)
    w_in = p["w_in"].astype(BF16)
    rw = 3 * RWKV_WIDTH
    zeros64 = jnp.zeros((64, RWKV_WIDTH), F32)

    def pad_lo(a):
        return jnp.concatenate([a, zeros64], axis=0).astype(BF16)

    def pad_hi(a):
        return jnp.concatenate([zeros64, a], axis=0).astype(BF16)

    row = lambda a: a.reshape(1, -1).astype(F32)
    wr = jnp.concatenate(
        [jnp.transpose(p["w_router_expert"], (1, 0, 2)).reshape(d, N_EXPERTS), p["w_router_group"],
         jnp.zeros((d, LANES - N_EXPERTS - N_GROUPS), F32)], axis=1)
    br = jnp.concatenate([p["b_router_expert"].reshape(-1), p["b_router_group"],
                          jnp.zeros((LANES - N_EXPERTS - N_GROUPS,), F32)]).reshape(1, LANES)
    return {
        "attn_norm": p["attn_norm"],
        "w_mla": w_in[:, offs[0]:offs[3]],
        "w_rw": w_in[:, offs[3]:offs[4]],
        "w_gates": w_in[:, offs[4]:offs[6]],
        "qa": row(p["q_a_norm"]), "kva": row(p["kv_a_norm"]),
        "wq": p["w_q_b"].astype(BF16), "wkv": p["w_kv_b"].astype(BF16),
        "qn": row(p["q_norm"]), "kn": row(p["k_norm"]),
        "shared": {"sp": row(p["shift_prev"]), "sn": row(p["shift_next"]), "k_k": row(p["k_k"]),
                   "k_a": row(p["k_a"]), "r_k": row(p["r_k"])},
        "fwd": {"w0": row(p["decay_w0_fwd"]), "w2": pad_lo(p["decay_w2_fwd"]),
                "a0": row(p["icl_a0_fwd"]), "a2": pad_lo(p["icl_a2_fwd"])},
        "bwd": {"w0": row(p["decay_w0_bwd"]), "w2": pad_hi(p["decay_w2_bwd"]),
                "a0": row(p["icl_a0_bwd"]), "a2": pad_hi(p["icl_a2_bwd"])},
        "tail": {"g2": p["gate_w2"].astype(BF16), "ln_w": row(p["ln_x_w"]), "ln_b": row(p["ln_x_b"])},
        "w_a": p["w_branch_mla"].astype(BF16), "w_b": p["w_branch_rwkv"].astype(BF16),
        "w_out": p["w_out"].astype(BF16), "ffn_norm": row(p["ffn_norm"]),
        "wr": wr, "br": br,
        "w_gate": p["w_expert_gate"].astype(BF16), "w_up": p["w_expert_up"].astype(BF16),
        "w_down": p["w_expert_down"].astype(BF16),
    }


def _rope_tables(length):
    pos = jnp.arange(length, dtype=F32)
    inv = ROPE_THETA ** (-jnp.arange(0, QK_ROPE, 2, dtype=F32) / QK_ROPE)
    ang = pos[:, None] * inv[None, :]
    cos, sin = jnp.cos(ang), jnp.sin(ang)
    return jnp.concatenate([cos, cos], axis=-1), jnp.concatenate([-sin, sin], axis=-1)


def _trunk(x, meta_tokens, wts):
    batch, seq, d = x.shape
    m = batch * seq
    x2 = x.reshape(m, d)
    meta = meta_tokens.astype(F32)
    cos, sin = _rope_tables(seq + N_META)

    zm_mla = _norm_matmul(meta, wts["attn_norm"], wts["w_mla"], tn=wts["w_mla"].shape[1])
    zm_rw = _norm_matmul(meta, wts["attn_norm"], wts["w_rw"], tn=1152)
    z_mla = _norm_matmul(x2, wts["attn_norm"], wts["w_mla"], tn=wts["w_mla"].shape[1])
    z_rw = _norm_matmul(x2, wts["attn_norm"], wts["w_rw"], tn=1152)
    gates = _norm_matmul(x2, wts["attn_norm"], wts["w_gates"], tn=1024)

    mla_w = (wts["qa"], wts["kva"], wts["wq"], wts["wkv"], wts["qn"], wts["kn"])
    q, k, v = _mla_prep(z_mla, batch, seq, cos[N_META:], sin[N_META:], *mla_w)
    _, km, vm = _mla_prep(zm_mla, 1, N_META, cos[:N_META], sin[:N_META], *mla_w)
    pad = ((0, 0), (0, META_ROWS - N_META), (0, 0))
    o_a = _attention(q, k, v, jnp.pad(km[0], pad), jnp.pad(vm[0], pad))
    o_a = o_a.reshape(m, MLA_HEADS * V_HEAD)

    lead = jnp.concatenate([jnp.zeros((CHUNK - N_META, zm_rw.shape[1]), F32), zm_rw], axis=0)
    o_b = _rwkv(z_rw.reshape(batch, seq, -1), lead, wts["shared"], wts["fwd"], wts["bwd"],
                wts["tail"])
    o_b = o_b.reshape(m, RWKV_WIDTH)

    merged = _merge(o_a, o_b, gates, wts["w_a"], wts["w_b"])
    h, n2, cmb = _out_router(x2, merged, wts["w_out"], wts["ffn_norm"], wts["wr"], wts["br"])
    out = _moe(n2, h, cmb, wts["w_gate"], wts["w_up"], wts["w_down"])
    return out.reshape(batch, seq, d)


def kernel(x_prompt, x_sample, meta_tokens, attn_norm, w_in, q_a_norm, w_q_b, kv_a_norm, w_kv_b, q_norm, k_norm, shift_prev, shift_next, decay_w0_fwd, decay_w2_fwd, decay_w0_bwd, decay_w2_bwd, icl_a0_fwd, icl_a2_fwd, icl_a0_bwd, icl_a2_bwd, gate_w2, k_k, k_a, r_k, ln_x_w, ln_x_b, w_branch_mla, w_branch_rwkv, w_out, ffn_norm, w_router_group, b_router_group, w_router_expert, b_router_expert, w_expert_gate, w_expert_up, w_expert_down):
    p = {
        "attn_norm": attn_norm, "w_in": w_in, "q_a_norm": q_a_norm, "w_q_b": w_q_b,
        "kv_a_norm": kv_a_norm, "w_kv_b": w_kv_b, "q_norm": q_norm, "k_norm": k_norm,
        "shift_prev": shift_prev, "shift_next": shift_next,
        "decay_w0_fwd": decay_w0_fwd, "decay_w2_fwd": decay_w2_fwd,
        "decay_w0_bwd": decay_w0_bwd, "decay_w2_bwd": decay_w2_bwd,
        "icl_a0_fwd": icl_a0_fwd, "icl_a2_fwd": icl_a2_fwd,
        "icl_a0_bwd": icl_a0_bwd, "icl_a2_bwd": icl_a2_bwd,
        "gate_w2": gate_w2, "k_k": k_k, "k_a": k_a, "r_k": r_k, "ln_x_w": ln_x_w, "ln_x_b": ln_x_b,
        "w_branch_mla": w_branch_mla, "w_branch_rwkv": w_branch_rwkv, "w_out": w_out,
        "ffn_norm": ffn_norm, "w_router_group": w_router_group, "b_router_group": b_router_group,
        "w_router_expert": w_router_expert, "b_router_expert": b_router_expert,
        "w_expert_gate": w_expert_gate, "w_expert_up": w_expert_up, "w_expert_down": w_expert_down,
    }
    wts = _prepare_weights({name: val[0] for name, val in p.items()})
    return _trunk(x_prompt, meta_tokens, wts), _trunk(x_sample, meta_tokens, wts)
```

```python
import functools
import math

import jax
import jax.numpy as jnp
from jax import lax
from jax.experimental import pallas as pl
from jax.experimental.pallas import tpu as pltpu

F32 = jnp.float32
BF16 = jnp.bfloat16

N_META = 16
RMS_EPS = 1e-6
MLA_HEADS = 8
QK_NOPE = 128
QK_ROPE = 64
QK_HEAD = QK_NOPE + QK_ROPE
V_HEAD = 128
V_WIDE = 256
Q_LORA = 512
KV_LORA = 512
ROPE_THETA = 10000.0
RWKV_HEADS = 16
RWKV_HEAD = 64
RWKV_WIDTH = RWKV_HEADS * RWKV_HEAD
GN_EPS = 64e-5
N_GROUPS = 4
EXPERTS_PER_GROUP = 8
N_EXPERTS = N_GROUPS * EXPERTS_PER_GROUP
D_EXPERT = 512

Z_RW = 3 * RWKV_WIDTH + 4 * 64 + 128
Z_MLA = 1152
Z_GATE = Z_RW + Z_MLA
Z_TILE = 2176

LANES = 128
CHUNK = 64
META_ROWS = 128
NEG_BIG = -1e30
MIB = 1024 * 1024


def _params(sem, vmem_mib):
    return pltpu.CompilerParams(dimension_semantics=sem, vmem_limit_bytes=vmem_mib * MIB)


def _dot(a, b):
    return jnp.dot(a, b, preferred_element_type=F32)


def _dot_nt(a, b):
    return lax.dot_general(a, b, (((1,), (1,)), ((), ())), preferred_element_type=F32)


def _dot_tn(a, b):
    return lax.dot_general(a, b, (((0,), (0,)), ((), ())), preferred_element_type=F32)


def _split2(x):
    hi = x.astype(BF16)
    lo = (x - hi.astype(F32)).astype(BF16)
    return hi, lo


def _dot_exact_lhs(a_bf16, x):
    hi, lo = _split2(x)
    return _dot(a_bf16, hi) + _dot(a_bf16, lo)


def _dot_wide(x, w):
    x1, x2 = _split2(x)
    w1, w2 = _split2(w)
    return _dot(x1, w1) + (_dot(x1, w2) + _dot(x2, w1))


def _norm_matmul_kernel(x_ref, g_ref, w_ref, o_ref, nb_ref):
    @pl.when(pl.program_id(1) == 0)
    def _():
        x = x_ref[...]
        y = x * lax.rsqrt(jnp.mean(x * x, axis=-1, keepdims=True) + RMS_EPS)
        nb_ref[...] = (y * g_ref[...]).astype(BF16)

    o_ref[...] = _dot(nb_ref[...], w_ref[...])


def _norm_matmul(x, gain, w, tn, tm=512):
    m, d = x.shape
    n = w.shape[1]
    tm = min(tm, m)
    assert m % tm == 0 and n % tn == 0
    return pl.pallas_call(
        _norm_matmul_kernel,
        out_shape=jax.ShapeDtypeStruct((m, n), F32),
        grid=(m // tm, n // tn),
        in_specs=[
            pl.BlockSpec((tm, d), lambda i, j: (i, 0)),
            pl.BlockSpec((1, d), lambda i, j: (0, 0)),
            pl.BlockSpec((d, tn), lambda i, j: (0, j)),
        ],
        out_specs=pl.BlockSpec((tm, tn), lambda i, j: (i, j)),
        scratch_shapes=[pltpu.VMEM((tm, d), BF16)],
        compiler_params=_params(("parallel", "arbitrary"), 48),
        name="norm_matmul",
    )(x, gain.reshape(1, d).astype(F32), w)


def _mla_prep_kernel(z_ref, cos_ref, sin_ref, qa_ref, kva_ref, wq_ref, wkv_ref, qn_ref, kn_ref,
                     q_ref, k_ref, v_ref):
    z = z_ref[...]
    cq = z[:, :Q_LORA]
    ckv = z[:, Q_LORA:Q_LORA + KV_LORA]
    kr = z[:, Q_LORA + KV_LORA:Q_LORA + KV_LORA + QK_ROPE]
    cos = cos_ref[...]
    sin = sin_ref[...]

    def rms(x, g):
        return x * lax.rsqrt(jnp.mean(x * x, axis=-1, keepdims=True) + RMS_EPS) * g

    def rope(x):
        swapped = jnp.concatenate([x[:, QK_ROPE // 2:], x[:, :QK_ROPE // 2]], axis=-1)
        return x * cos + swapped * sin

    q_all = _dot(rms(cq, qa_ref[...]).astype(BF16), wq_ref[...])
    kv_all = _dot(rms(ckv, kva_ref[...]).astype(BF16), wkv_ref[...])
    qn = qn_ref[...]
    kn = kn_ref[...]
    scale = QK_HEAD ** -0.5 * math.log2(math.e)
    ones_col = (lax.broadcasted_iota(jnp.int32, (z.shape[0], V_WIDE - V_HEAD), 1) == 0).astype(BF16)
    kr_sq = jnp.sum(kr * kr, axis=-1, keepdims=True)
    kr_rot = rope(kr * kn[:, QK_NOPE:])
    for h in range(MLA_HEADS):
        qh = q_all[:, h * QK_HEAD:(h + 1) * QK_HEAD]
        qh = qh * (lax.rsqrt(jnp.mean(qh * qh, axis=-1, keepdims=True) + RMS_EPS) * scale) * qn
        q_ref[0, h, :, :QK_NOPE] = qh[:, :QK_NOPE].astype(BF16)
        q_ref[0, h, :, QK_NOPE:] = rope(qh[:, QK_NOPE:]).astype(BF16)
        base = h * (QK_NOPE + V_HEAD)
        k_nope = kv_all[:, base:base + QK_NOPE]
        ms = (jnp.sum(k_nope * k_nope, axis=-1, keepdims=True) + kr_sq) * (1.0 / QK_HEAD)
        inv = lax.rsqrt(ms + RMS_EPS)
        k_ref[0, h, :, :QK_NOPE] = (k_nope * inv * kn[:, :QK_NOPE]).astype(BF16)
        k_ref[0, h, :, QK_NOPE:] = (kr_rot * inv).astype(BF16)
        v_ref[0, h, :, :V_HEAD] = kv_all[:, base + QK_NOPE:base + QK_NOPE + V_HEAD].astype(BF16)
        v_ref[0, h, :, V_HEAD:] = ones_col


def _mla_prep(z, batch, seq, cos, sin, qa, kva, wq, wkv, qn, kn, tm=512):
    tm = min(tm, seq)
    assert seq % tm == 0 and Z_RW % Z_MLA == 0
    nt = seq // tm
    full = lambda b, i: (0, 0)
    return pl.pallas_call(
        _mla_prep_kernel,
        out_shape=(
            jax.ShapeDtypeStruct((batch, MLA_HEADS, seq, QK_HEAD), BF16),
            jax.ShapeDtypeStruct((batch, MLA_HEADS, seq, QK_HEAD), BF16),
            jax.ShapeDtypeStruct((batch, MLA_HEADS, seq, V_WIDE), BF16),
        ),
        grid=(batch, nt),
        in_specs=[
            pl.BlockSpec((tm, Z_MLA), lambda b, i: (b * nt + i, Z_RW // Z_MLA)),
            pl.BlockSpec((tm, QK_ROPE), lambda b, i: (i, 0)),
            pl.BlockSpec((tm, QK_ROPE), lambda b, i: (i, 0)),
            pl.BlockSpec((1, Q_LORA), full),
            pl.BlockSpec((1, KV_LORA), full),
            pl.BlockSpec(wq.shape, full),
            pl.BlockSpec(wkv.shape, full),
            pl.BlockSpec((1, QK_HEAD), full),
            pl.BlockSpec((1, QK_HEAD), full),
        ],
        out_specs=(
            pl.BlockSpec((1, MLA_HEADS, tm, QK_HEAD), lambda b, i: (b, 0, i, 0)),
            pl.BlockSpec((1, MLA_HEADS, tm, QK_HEAD), lambda b, i: (b, 0, i, 0)),
            pl.BlockSpec((1, MLA_HEADS, tm, V_WIDE), lambda b, i: (b, 0, i, 0)),
        ),
        compiler_params=_params(("parallel", "parallel"), 48),
        name="mla_prep",
    )(z, cos, sin, qa, kva, wq, wkv, qn, kn)


def _attention_kernel(q_ref, k_ref, v_ref, km_ref, vm_ref, o_ref, *, tk, n_split):
    q = q_ref[0, 0]
    tq = q.shape[0]
    seq = k_ref.shape[2]

    rows = tq // n_split
    qs = [q[g * rows:(g + 1) * rows] for g in range(n_split)]
    ss = [_dot_nt(x, km_ref[0]) for x in qs]
    lane = lax.broadcasted_iota(jnp.int32, ss[0].shape, 1)
    ss = [jnp.where(lane < N_META, s, NEG_BIG) for s in ss]
    ms = [jnp.max(s, axis=-1, keepdims=True) for s in ss]
    accs = [_dot(jnp.exp2(s - m).astype(BF16), vm_ref[0]) for s, m in zip(ss, ms)]

    def body(c, carry):
        ms, accs = carry
        off = pl.multiple_of(c * tk, tk)
        k = k_ref[0, 0, pl.ds(off, tk), :]
        v = v_ref[0, 0, pl.ds(off, tk), :]
        ss = [_dot_nt(x, k) for x in qs]
        m_new = [jnp.maximum(m, jnp.max(s, axis=-1, keepdims=True)) for m, s in zip(ms, ss)]
        ps = [jnp.exp2(s - m).astype(BF16) for s, m in zip(ss, m_new)]
        accs = [jnp.exp2(m - mn) * a + _dot(p, v) for m, mn, a, p in zip(ms, m_new, accs, ps)]
        return m_new, accs

    ms, accs = lax.fori_loop(0, seq // tk, body, (ms, accs))
    for g in range(n_split):
        acc = accs[g]
        o_ref[0, g * rows:(g + 1) * rows, :] = (
            acc[:, :V_HEAD] / acc[:, V_HEAD:V_HEAD + 1]).astype(BF16)


def _attention(q, k, v, k_meta, v_meta, tq=1024, tk=2048, n_split=4):
    batch, heads, seq, _ = q.shape
    tq = min(tq, seq)
    tk = min(tk, seq)
    assert seq % tq == 0 and seq % tk == 0 and tq % (16 * n_split) == 0
    return pl.pallas_call(
        functools.partial(_attention_kernel, tk=tk, n_split=n_split),
        out_shape=jax.ShapeDtypeStruct((batch, seq, heads * V_HEAD), BF16),
        grid=(batch, heads, seq // tq),
        in_specs=[
            pl.BlockSpec((1, 1, tq, QK_HEAD), lambda b, h, i: (b, h, i, 0)),
            pl.BlockSpec((1, 1, seq, QK_HEAD), lambda b, h, i: (b, h, 0, 0)),
            pl.BlockSpec((1, 1, seq, V_WIDE), lambda b, h, i: (b, h, 0, 0)),
            pl.BlockSpec((1, META_ROWS, QK_HEAD), lambda b, h, i: (h, 0, 0)),
            pl.BlockSpec((1, META_ROWS, V_WIDE), lambda b, h, i: (h, 0, 0)),
        ],
        out_specs=pl.BlockSpec((1, tq, V_HEAD), lambda b, h, i: (b, i, h)),
        compiler_params=_params(("parallel", "parallel", "arbitrary"), 48),
        name="mla_attention",
    )(q, k, v, k_meta, v_meta)


def _head_sum(x, ones_blk):
    xb = x.astype(BF16)
    cols = [_dot(xb[:, c * LANES:(c + 1) * LANES], ones_blk) for c in range(x.shape[1] // LANES)]
    return jnp.concatenate(cols, axis=-1)


def _block_diag(x, bd_mask):
    return jnp.where(bd_mask, jnp.concatenate([x, x], axis=0), jnp.zeros((), x.dtype))


def _tri_inverse(a_list, row, col, bd_mask):
    def same_block(n):
        return (row // n) == (col // n)

    eye = (row == col).astype(F32)
    apow = [jnp.where(same_block(16), a, 0.0) for a in a_list]
    t = [eye - a for a in apow]
    for _ in range(3):
        apow_b = [a.astype(BF16) for a in apow]
        apow = [_dot(a, _block_diag(a, bd_mask)) for a in apow_b]
        t = [ti + _dot(ti.astype(BF16), _block_diag(a.astype(BF16), bd_mask))
             for ti, a in zip(t, apow)]
    n = 32
    while n <= CHUNK:
        off_mask = same_block(n) & jnp.logical_not(same_block(n // 2))
        t_bd = [_block_diag(ti.astype(BF16), bd_mask) for ti in t]
        inner = [_dot(jnp.where(off_mask, a, 0.0).astype(BF16), tb).astype(BF16)
                 for a, tb in zip(a_list, t_bd)]
        t = [ti - _dot(ti.astype(BF16), _block_diag(x, bd_mask)) for ti, x in zip(t, inner)]
        n *= 2
    return t


def _rwkv_kernel(*refs, reverse, n_lead, nb):
    if reverse:
        (zs_ref, kkn_ref, yf_ref, bf_ref, w0_ref, w2_ref, a0_ref, a2_ref, ka_ref, rk_ref,
         g2_ref, lnw_ref, lnb_ref, o_ref, st_ref) = refs
    else:
        (z_ref, zp_ref, zn_ref, lead_ref, sp_ref, sn_ref, w0_ref, w2_ref, a0_ref, a2_ref,
         kk_ref, ka_ref, rk_ref, y_ref, bon_ref, zs_ref, kkn_ref, st_ref) = refs
    c = pl.program_id(1)
    nc = pl.num_programs(1)
    phys = (nc - c) if reverse else c

    @pl.when(c == 0)
    def _():
        st_ref[...] = jnp.zeros_like(st_ref)

    w = RWKV_WIDTH
    rows = lax.broadcasted_iota(jnp.int32, (CHUNK, 1), 0)
    lane_r = lax.broadcasted_iota(jnp.int32, (LANES, LANES), 0)
    lane_c = lax.broadcasted_iota(jnp.int32, (LANES, LANES), 1)
    bd_mask = (lane_r // RWKV_HEAD) == (lane_c // RWKV_HEAD)
    ones_blk = bd_mask.astype(BF16)
    row = lax.broadcasted_iota(jnp.int32, (CHUNK, LANES), 0)
    lane = lax.broadcasted_iota(jnp.int32, (CHUNK, LANES), 1)
    col = lane % CHUNK
    first_head = lane < RWKV_HEAD
    if reverse:
        incl = col >= row
        strict = col > row
    else:
        incl = col <= row
        strict = col < row
    tri = incl[:, :CHUNK].astype(BF16)

    def operands(i):
        if reverse:
            zs = zs_ref[i]
            kkn = kkn_ref[i]
            r, k, v = zs[:, :w], zs[:, w:2 * w], zs[:, 2 * w:3 * w]
        else:
            z = jnp.where(phys == 0, lead_ref[...], z_ref[i])
            prev_row = jnp.where(phys == 1, lead_ref[CHUNK - 1:CHUNK, :], zp_ref[i, 7:8, :])
            prev_row = jnp.where(phys > 0, prev_row, 0.0)
            next_row = jnp.where(phys < nc - 1, zn_ref[i, 0:1, :], 0.0)
            z_prev = jnp.where(rows == 0, prev_row, pltpu.roll(z, 1, axis=0))
            z_next = jnp.where(rows == CHUNK - 1, next_row, pltpu.roll(z, CHUNK - 1, axis=0))
            zs = z + sp_ref[...] * (z_prev - z) + sn_ref[...] * (z_next - z)
            zs_ref[i] = zs
            live = jnp.logical_or(phys > 0, rows >= n_lead).astype(F32)
            r = zs[:, :w] * live
            k = zs[:, w:2 * w] * live
            v = zs[:, 2 * w:3 * w] * live
            kk = k * kk_ref[...]
            kkn = kk * jnp.minimum(lax.rsqrt(_head_sum(kk * kk, ones_blk)), 1e12)
            kkn_ref[i] = kkn
        wd = zs[:, 3 * w:3 * w + LANES]
        ad = zs[:, 3 * w + LANES:3 * w + 2 * LANES]
        w_raw = w0_ref[...] + _dot(jnp.tanh(wd).astype(BF16), w2_ref[...])
        lw = (-math.exp(-0.5)) * jax.nn.sigmoid(w_raw)
        icl = jax.nn.sigmoid(a0_ref[...] + _dot(ad.astype(BF16), a2_ref[...]))
        k_rep = k * (1.0 + (icl - 1.0) * ka_ref[...])
        bvec = kkn * icl
        bonus = _head_sum(r * k_rep * rk_ref[...], ones_blk) * v
        l_in = _dot_exact_lhs(tri, lw)
        l_tot = l_in[0:1, :] if reverse else l_in[CHUNK - 1:CHUNK, :]
        e_neg = jnp.exp(-l_in)
        e_rem = jnp.exp(l_tot - l_in)
        ops = {
            "kd": (kkn * jnp.exp(l_in - lw)).astype(BF16), "rd": (r * jnp.exp(l_in)).astype(BF16),
            "bn": (bvec * e_neg).astype(BF16), "kn": (k_rep * e_neg).astype(BF16),
            "bh": (bvec * e_rem).astype(BF16), "kh": (k_rep * e_rem).astype(BF16),
            "v": v.astype(BF16),
        }
        p_tot_t = jnp.transpose(jnp.broadcast_to(jnp.exp(l_tot), (8, w)))
        return ops, p_tot_t, bonus, zs

    per_seq = [operands(i) for i in range(nb)]

    n_pairs = RWKV_HEADS // 2
    items = [(i, p) for i in range(nb) for p in range(n_pairs)]
    bd = lambda x: _block_diag(x, bd_mask)
    diag_blocks = lambda x: jnp.where(first_head, x[:RWKV_HEAD], x[RWKV_HEAD:])
    kd_p, rd_p, bn_p, kn_p, bh_p, kh_p, v_p = (
        [per_seq[i][0][name][:, p * LANES:(p + 1) * LANES] for i, p in items]
        for name in ("kd", "rd", "bn", "kn", "bh", "kh", "v"))
    v_bd = [bd(x) for x in v_p]
    kr_p = [jnp.concatenate([a, b], axis=0) for a, b in zip(kd_p, rd_p)]
    prod_b = [_dot_nt(x, bd(y)) for x, y in zip(kr_p, bn_p)]
    prod_k = [_dot_nt(x, bd(y)) for x, y in zip(kr_p, kn_p)]
    a_ab = [jnp.where(strict, x[:CHUNK], 0.0) for x in prod_b]
    m_rb = [jnp.where(incl, x[CHUNK:], 0.0).astype(BF16) for x in prod_b]
    a_ak = [jnp.where(strict, x[:CHUNK], 0.0).astype(BF16) for x in prod_k]
    m_rk = [jnp.where(incl, x[CHUNK:], 0.0).astype(BF16) for x in prod_k]
    akv = [_dot(a, vv).astype(BF16) for a, vv in zip(a_ak, v_bd)]
    t_inv = [t.astype(BF16) for t in _tri_inverse(a_ab, row, col, bd_mask)]
    wmat = [(-_dot(t, bd(x))).astype(BF16) for t, x in zip(t_inv, kd_p)]
    u0 = [(-_dot(t, bd(x))).astype(BF16) for t, x in zip(t_inv, akv)]
    u0_bd = [bd(x) for x in u0]
    st = [st_ref[i * n_pairs + p] for i, p in items]
    st_bd = [bd(s.astype(BF16)) for s in st]
    q_mat = [(x.astype(F32) + _dot(m, bd(wm))).astype(BF16)
             for x, m, wm in zip(rd_p, m_rb, wmat)]
    ys = [_dot(qm, sb) + _dot(mb, u) + _dot(mk, vv)
          for qm, sb, mb, u, mk, vv in zip(q_mat, st_bd, m_rb, u0_bd, m_rk, v_bd)]
    g_mat = [diag_blocks(_dot_tn(b, wm)).astype(BF16) for b, wm in zip(bh_p, wmat)]
    h_mat = [diag_blocks(_dot_tn(b, u) + _dot_tn(kx, vv))
             for b, u, kx, vv in zip(bh_p, u0, kh_p, v_p)]
    for n, (i, p) in enumerate(items):
        p_tot_t = per_seq[i][1]
        decay = jnp.where(first_head, p_tot_t[2 * p * RWKV_HEAD:(2 * p + 1) * RWKV_HEAD, 0:1],
                          p_tot_t[(2 * p + 1) * RWKV_HEAD:(2 * p + 2) * RWKV_HEAD, 0:1])
        st_ref[i * n_pairs + p] = decay * st[n] + _dot(g_mat[n], st_bd[n]) + h_mat[n]

    for i in range(nb):
        y = jnp.concatenate(ys[i * n_pairs:(i + 1) * n_pairs], axis=-1)
        _, _, bonus, zs = per_seq[i]
        if not reverse:
            y_ref[i] = y
            bon_ref[i] = bonus
        else:
            y = y + yf_ref[i]
            mu = _head_sum(y, ones_blk) * (1.0 / RWKV_HEAD)
            yc = y - mu
            var = _head_sum(yc * yc, ones_blk) * (1.0 / RWKV_HEAD)
            yn = (yc * lax.rsqrt(var + GN_EPS) * lnw_ref[...] + lnb_ref[...]
                  + (bonus + bf_ref[i]))
            gd = zs[:, 3 * w + 2 * LANES:3 * w + 3 * LANES]
            g = _dot(jax.nn.sigmoid(gd).astype(BF16), g2_ref[...])
            o_ref[i] = (yn * g).astype(BF16)


def _rwkv(z_rw, lead, shared, fwd_p, bwd_p, tail, nb=2):
    batch, seq, _ = z_rw.shape
    zc = Z_RW
    nb = min(nb, batch)
    assert batch % nb == 0 and seq % CHUNK == 0
    ncx = seq // CHUNK
    lp = seq + CHUNK
    sub = CHUNK // 8
    w = RWKV_WIDTH
    n_lead = CHUNK - N_META
    full = lambda b, c: (0, 0)
    row_w = pl.BlockSpec((1, w), full)
    mat_w = pl.BlockSpec((LANES, w), full)
    state = pltpu.VMEM((nb * RWKV_HEADS // 2, RWKV_HEAD, 2 * RWKV_HEAD), F32)

    chunk = lambda b, c: (b, c, 0)
    fwd_specs = [
        pl.BlockSpec((nb, CHUNK, zc), lambda b, c: (b, jnp.maximum(c - 1, 0), 0)),
        pl.BlockSpec((nb, 8, zc), lambda b, c: (b, jnp.maximum((c - 1) * sub - 1, 0), 0)),
        pl.BlockSpec((nb, 8, zc), lambda b, c: (b, jnp.minimum(c * sub, ncx * sub - 1), 0)),
        pl.BlockSpec((CHUNK, zc), full),
        pl.BlockSpec((1, zc), full), pl.BlockSpec((1, zc), full),
        row_w, mat_w, row_w, mat_w, row_w, row_w, row_w,
    ]
    fwd_args = [z_rw, z_rw, z_rw, lead, shared["sp"], shared["sn"], fwd_p["w0"], fwd_p["w2"],
                fwd_p["a0"], fwd_p["a2"], shared["k_k"], shared["k_a"], shared["r_k"]]
    y_f, bon_f, zs, kkn = pl.pallas_call(
        functools.partial(_rwkv_kernel, reverse=False, n_lead=n_lead, nb=nb),
        out_shape=(jax.ShapeDtypeStruct((batch, lp, w), F32),
                   jax.ShapeDtypeStruct((batch, lp, w), F32),
                   jax.ShapeDtypeStruct((batch, lp, zc), F32),
                   jax.ShapeDtypeStruct((batch, lp, w), F32)),
        grid=(batch // nb, ncx + 1),
        in_specs=fwd_specs,
        out_specs=(pl.BlockSpec((nb, CHUNK, w), chunk), pl.BlockSpec((nb, CHUNK, w), chunk),
                   pl.BlockSpec((nb, CHUNK, zc), chunk), pl.BlockSpec((nb, CHUNK, w), chunk)),
        scratch_shapes=[state],
        compiler_params=_params(("parallel", "arbitrary"), 48),
        name="rwkv7_fwd",
    )(*fwd_args)

    rchunk = lambda b, c: (b, ncx - c, 0)
    bwd_specs = [
        pl.BlockSpec((nb, CHUNK, zc), rchunk), pl.BlockSpec((nb, CHUNK, w), rchunk),
        pl.BlockSpec((nb, CHUNK, w), rchunk), pl.BlockSpec((nb, CHUNK, w), rchunk),
        row_w, mat_w, row_w, mat_w, row_w, row_w, mat_w, row_w, row_w,
    ]
    bwd_args = [zs, kkn, y_f, bon_f, bwd_p["w0"], bwd_p["w2"], bwd_p["a0"], bwd_p["a2"],
                shared["k_a"], shared["r_k"], tail["g2"], tail["ln_w"], tail["ln_b"]]
    return pl.pallas_call(
        functools.partial(_rwkv_kernel, reverse=True, n_lead=n_lead, nb=nb),
        out_shape=jax.ShapeDtypeStruct((batch, seq, w), BF16),
        grid=(batch // nb, ncx),
        in_specs=bwd_specs,
        out_specs=pl.BlockSpec((nb, CHUNK, w), lambda b, c: (b, ncx - 1 - c, 0)),
        scratch_shapes=[state],
        compiler_params=_params(("parallel", "arbitrary"), 48),
        name="rwkv7_bwd",
    )(*bwd_args)


def _merge_kernel(oa_ref, ob_ref, ga_ref, gb_ref, wa_ref, wb_ref, o_ref):
    a = _dot(oa_ref[...], wa_ref[...])
    b = _dot(ob_ref[...], wb_ref[...])
    o_ref[...] = (jax.nn.sigmoid(ga_ref[...]) * a + jax.nn.sigmoid(gb_ref[...]) * b).astype(BF16)


def _merge(o_a, o_b, z, w_a, w_b, tm=512, tn=512):
    m, d = o_a.shape[0], w_a.shape[1]
    tm = min(tm, m)
    assert Z_GATE % tn == 0 and d % tn == 0
    nj = d // tn
    g0 = Z_GATE // tn
    return pl.pallas_call(
        _merge_kernel,
        out_shape=jax.ShapeDtypeStruct((m, d), BF16),
        grid=(m // tm, nj),
        in_specs=[
            pl.BlockSpec((tm, o_a.shape[1]), lambda i, j: (i, 0)),
            pl.BlockSpec((tm, o_b.shape[1]), lambda i, j: (i, 0)),
            pl.BlockSpec((tm, tn), lambda i, j: (i, g0 + j)),
            pl.BlockSpec((tm, tn), lambda i, j: (i, g0 + nj + j)),
            pl.BlockSpec((w_a.shape[0], tn), lambda i, j: (0, j)),
            pl.BlockSpec((w_b.shape[0], tn), lambda i, j: (0, j)),
        ],
        out_specs=pl.BlockSpec((tm, tn), lambda i, j: (i, j)),
        compiler_params=_params(("parallel", "parallel"), 48),
        name="branch_merge",
    )(o_a, o_b, z, z, w_a, w_b)


def _out_router_kernel(x_ref, mg_ref, wo_ref, g_ref, wr_ref, br_ref, h_ref, n2_ref, cmb_ref):
    h = x_ref[...] + _dot(mg_ref[...], wo_ref[...])
    h_ref[...] = h
    n2 = h * lax.rsqrt(jnp.mean(h * h, axis=-1, keepdims=True) + RMS_EPS) * g_ref[...]
    n2_ref[...] = n2.astype(BF16)
    logits = _dot_wide(n2, wr_ref[...]) + br_ref[...]
    lane = lax.broadcasted_iota(jnp.int32, logits.shape, 1)
    big = jnp.int32(1 << 20)
    g_mask = (lane >= N_EXPERTS) & (lane < N_EXPERTS + N_GROUPS)
    gl = jnp.where(g_mask, logits, NEG_BIG)
    g_max = jnp.max(gl, axis=-1, keepdims=True)
    pg_top = 1.0 / jnp.sum(jnp.where(g_mask, jnp.exp(gl - g_max), 0.0), axis=-1, keepdims=True)
    gi = jnp.min(jnp.where(gl == g_max, lane, big), axis=-1, keepdims=True) - N_EXPERTS
    e_mask = (lane >= gi * EXPERTS_PER_GROUP) & (lane < (gi + 1) * EXPERTS_PER_GROUP)
    el = jnp.where(e_mask, logits, NEG_BIG)
    m1 = jnp.max(el, axis=-1, keepdims=True)
    i1 = jnp.min(jnp.where(el == m1, lane, big), axis=-1, keepdims=True)
    el2 = jnp.where(lane == i1, NEG_BIG, el)
    m2 = jnp.max(el2, axis=-1, keepdims=True)
    i2 = jnp.min(jnp.where(el2 == m2, lane, big), axis=-1, keepdims=True)
    e21 = jnp.exp(m2 - m1)
    gate1 = pg_top / (1.0 + e21)
    gate2 = pg_top * e21 / (1.0 + e21)
    cmb_ref[...] = jnp.where(lane == i1, gate1, 0.0) + jnp.where(lane == i2, gate2, 0.0)


def _out_router(x, merged, w_out, gain, w_r, b_r, tm=256):
    m, d = x.shape
    tm = min(tm, m)
    full = lambda i: (0, 0)
    row = lambda i: (i, 0)
    return pl.pallas_call(
        _out_router_kernel,
        out_shape=(jax.ShapeDtypeStruct((m, d), F32), jax.ShapeDtypeStruct((m, d), BF16),
                   jax.ShapeDtypeStruct((m, LANES), F32)),
        grid=(m // tm,),
        in_specs=[
            pl.BlockSpec((tm, d), row), pl.BlockSpec((tm, d), row),
            pl.BlockSpec((d, d), full), pl.BlockSpec((1, d), full),
            pl.BlockSpec((d, LANES), full), pl.BlockSpec((1, LANES), full),
        ],
        out_specs=(pl.BlockSpec((tm, d), row), pl.BlockSpec((tm, d), row),
                   pl.BlockSpec((tm, LANES), row)),
        compiler_params=_params(("parallel",), 48),
        name="out_proj_router",
    )(x, merged, w_out, gain, w_r, b_r)


def _moe_kernel(x_ref, h_ref, cmb_ref, tri_ref, wg_ref, wu_ref, wd_ref, o_ref, pos_ref, cmbt_ref,
                hot_ref, ye_ref, cnt_ref, *, rb):
    e = pl.program_id(1)
    tm, d = x_ref.shape
    cb = min(d, 256)
    seg = tri_ref.shape[0]

    @pl.when(e == 0)
    def _():
        pltpu.sync_copy(h_ref.at[pl.ds(pl.multiple_of(pl.program_id(0) * tm, tm), tm), :], o_ref)
        base = jnp.zeros((LANES, 1), F32)
        for s in range(tm // seg):
            cmb_t = jnp.transpose(cmb_ref[s * seg:(s + 1) * seg, :])
            cmbt_ref[:, s * seg:(s + 1) * seg] = cmb_t
            pos = _dot((cmb_t > 0.0).astype(BF16), tri_ref[...]) + base
            pos_ref[:, s * seg:(s + 1) * seg] = pos
            base = pos[:, seg - 1:seg]
        hot_ref[...] = jnp.zeros_like(hot_ref)
        ye_ref[...] = jnp.zeros_like(ye_ref)
        cnt_ref[0] = 0

    def flush():
        hot = hot_ref[...]
        for j in range(d // cb):
            o_ref[:, j * cb:(j + 1) * cb] += _dot_tn(hot, ye_ref[:, j * cb:(j + 1) * cb])

    pos = pos_ref[pl.ds(e, 1), :]
    gate = cmbt_ref[pl.ds(e, 1), :]
    routed = gate > 0.0
    count = jnp.max(pos).astype(jnp.int32)
    slot = lax.broadcasted_iota(jnp.int32, (rb, tm), 0)
    pos_i = pos.astype(jnp.int32) - 1

    def body(blk, carry):
        hit = routed & (pos_i == slot + blk * rb)
        onehot = jnp.where(hit, 1.0, 0.0).astype(BF16)
        g_col = jnp.sum(jnp.where(hit, gate, 0.0), axis=-1, keepdims=True)
        xe = _dot(onehot, x_ref[...]).astype(BF16)
        hg = _dot(xe, wg_ref[0])
        hu = _dot(xe, wu_ref[0])
        act = (hg * jax.nn.sigmoid(hg)) * hu * g_col
        half = cnt_ref[0] % 2
        off = pl.multiple_of(half * rb, rb)
        hot_ref[pl.ds(off, rb), :] = onehot
        ye_ref[pl.ds(off, rb), :] = _dot(act.astype(BF16), wd_ref[0]).astype(BF16)
        cnt_ref[0] = cnt_ref[0] + 1
        pl.when(half == 1)(flush)
        return carry

    lax.fori_loop(0, (count + rb - 1) // rb, body, 0)

    @pl.when((e == pl.num_programs(1) - 1) & (cnt_ref[0] % 2 == 1))
    def _():
        hot_ref[rb:, :] = jnp.zeros((rb, tm), BF16)
        flush()


def _moe(n2, h, cmb, w_gate, w_up, w_down, tm=2048, rb=128, seg=1024):
    m, d = n2.shape
    tm = min(tm, m)
    seg = min(seg, tm)
    assert m % tm == 0 and tm % seg == 0
    n_e, _, f = w_gate.shape
    row = lambda i, e: (i, 0)
    once = pl.Buffered(1)
    idx = jnp.arange(seg)
    tri = (idx[:, None] <= idx[None, :]).astype(BF16)
    return pl.pallas_call(
        functools.partial(_moe_kernel, rb=rb),
        out_shape=jax.ShapeDtypeStruct((m, d), F32),
        grid=(m // tm, n_e),
        in_specs=[
            pl.BlockSpec((tm, d), row, pipeline_mode=once),
            pl.BlockSpec(memory_space=pl.ANY),
            pl.BlockSpec((tm, LANES), row, pipeline_mode=once),
            pl.BlockSpec((seg, seg), lambda i, e: (0, 0), pipeline_mode=once),
            pl.BlockSpec((1, d, f), lambda i, e: (e, 0, 0)),
            pl.BlockSpec((1, d, f), lambda i, e: (e, 0, 0)),
            pl.BlockSpec((1, f, d), lambda i, e: (e, 0, 0)),
        ],
        out_specs=pl.BlockSpec((tm, d), row, pipeline_mode=once),
        scratch_shapes=[pltpu.VMEM((LANES, tm), F32), pltpu.VMEM((LANES, tm), F32),
                        pltpu.VMEM((2 * rb, tm), BF16), pltpu.VMEM((2 * rb, d), BF16),
                        pltpu.SMEM((1,), jnp.int32)],
        compiler_params=_params(("parallel", "arbitrary"), 56),
        name="moe_experts",
    )(n2, h, cmb, tri, w_gate, w_up, w_down)


def _prepare_weights(p):
    d = p["w_in"].shape[0]
    splits = [Q_LORA, KV_LORA, QK_ROPE, 3 * RWKV_WIDTH + 4 * 64 + 128, d, d]
    offs = [0]
    for s in splits:
        offs.append(offs[-1] + s)
    w_in = p["w_in"].astype(BF16)
    rw = 3 * RWKV_WIDTH
    zeros64 = jnp.zeros((64, RWKV_WIDTH), F32)

    def pad_lo(a):
        return jnp.concatenate([a, zeros64], axis=0).astype(BF16)

    def pad_hi(a):
        return jnp.concatenate([zeros64, a], axis=0).astype(BF16)

    row = lambda a: a.reshape(1, -1).astype(F32)
    wr = jnp.concatenate(
        [jnp.transpose(p["w_router_expert"], (1, 0, 2)).reshape(d, N_EXPERTS), p["w_router_group"],
         jnp.zeros((d, LANES - N_EXPERTS - N_GROUPS), F32)], axis=1)
    br = jnp.concatenate([p["b_router_expert"].reshape(-1), p["b_router_group"],
                          jnp.zeros((LANES - N_EXPERTS - N_GROUPS,), F32)]).reshape(1, LANES)
    return {
        "attn_norm": p["attn_norm"],
        "w_all": jnp.concatenate(
            [w_in[:, offs[3]:offs[4]], w_in[:, offs[0]:offs[3]],
             jnp.zeros((d, Z_MLA - (offs[3] - offs[0])), BF16), w_in[:, offs[4]:offs[6]]], axis=1),
        "qa": row(p["q_a_norm"]), "kva": row(p["kv_a_norm"]),
        "wq": p["w_q_b"].astype(BF16), "wkv": p["w_kv_b"].astype(BF16),
        "qn": row(p["q_norm"]), "kn": row(p["k_norm"]),
        "shared": {"sp": row(p["shift_prev"]), "sn": row(p["shift_next"]), "k_k": row(p["k_k"]),
                   "k_a": row(p["k_a"]), "r_k": row(p["r_k"])},
        "fwd": {"w0": row(p["decay_w0_fwd"]), "w2": pad_lo(p["decay_w2_fwd"]),
                "a0": row(p["icl_a0_fwd"]), "a2": pad_lo(p["icl_a2_fwd"])},
        "bwd": {"w0": row(p["decay_w0_bwd"]), "w2": pad_hi(p["decay_w2_bwd"]),
                "a0": row(p["icl_a0_bwd"]), "a2": pad_hi(p["icl_a2_bwd"])},
        "tail": {"g2": p["gate_w2"].astype(BF16), "ln_w": row(p["ln_x_w"]), "ln_b": row(p["ln_x_b"])},
        "w_a": p["w_branch_mla"].astype(BF16), "w_b": p["w_branch_rwkv"].astype(BF16),
        "w_out": p["w_out"].astype(BF16), "ffn_norm": row(p["ffn_norm"]),
        "wr": wr, "br": br,
        "w_gate": p["w_expert_gate"].astype(BF16), "w_up": p["w_expert_up"].astype(BF16),
        "w_down": p["w_expert_down"].astype(BF16),
    }


def _rope_tables(length):
    pos = jnp.arange(length, dtype=F32)
    inv = ROPE_THETA ** (-jnp.arange(0, QK_ROPE, 2, dtype=F32) / QK_ROPE)
    ang = pos[:, None] * inv[None, :]
    cos, sin = jnp.cos(ang), jnp.sin(ang)
    return jnp.concatenate([cos, cos], axis=-1), jnp.concatenate([-sin, sin], axis=-1)


def _trunk(x, meta_tokens, wts):
    batch, seq, d = x.shape
    m = batch * seq
    x2 = x.reshape(m, d)
    meta = meta_tokens.astype(F32)
    cos, sin = _rope_tables(seq + N_META)

    zm = _norm_matmul(meta, wts["attn_norm"], wts["w_all"], tn=Z_TILE)
    z = _norm_matmul(x2, wts["attn_norm"], wts["w_all"], tn=Z_TILE)

    mla_w = (wts["qa"], wts["kva"], wts["wq"], wts["wkv"], wts["qn"], wts["kn"])
    q, k, v = _mla_prep(z, batch, seq, cos[N_META:], sin[N_META:], *mla_w)
    _, km, vm = _mla_prep(zm, 1, N_META, cos[:N_META], sin[:N_META], *mla_w)
    pad = ((0, 0), (0, META_ROWS - N_META), (0, 0))
    o_a = _attention(q, k, v, jnp.pad(km[0], pad), jnp.pad(vm[0], pad))
    o_a = o_a.reshape(m, MLA_HEADS * V_HEAD)

    lead = jnp.concatenate([jnp.zeros((CHUNK - N_META, Z_RW), F32), zm[:, :Z_RW]], axis=0)
    o_b = _rwkv(z.reshape(batch, seq, -1), lead, wts["shared"], wts["fwd"], wts["bwd"],
                wts["tail"])
    o_b = o_b.reshape(m, RWKV_WIDTH)

    merged = _merge(o_a, o_b, z, wts["w_a"], wts["w_b"])
    h, n2, cmb = _out_router(x2, merged, wts["w_out"], wts["ffn_norm"], wts["wr"], wts["br"])
    out = _moe(n2, h, cmb, wts["w_gate"], wts["w_up"], wts["w_down"])
    return out.reshape(batch, seq, d)


def kernel(x_prompt, x_sample, meta_tokens, attn_norm, w_in, q_a_norm, w_q_b, kv_a_norm, w_kv_b, q_norm, k_norm, shift_prev, shift_next, decay_w0_fwd, decay_w2_fwd, decay_w0_bwd, decay_w2_bwd, icl_a0_fwd, icl_a2_fwd, icl_a0_bwd, icl_a2_bwd, gate_w2, k_k, k_a, r_k, ln_x_w, ln_x_b, w_branch_mla, w_branch_rwkv, w_out, ffn_norm, w_router_group, b_router_group, w_router_expert, b_router_expert, w_expert_gate, w_expert_up, w_expert_down):
    p = {
        "attn_norm": attn_norm, "w_in": w_in, "q_a_norm": q_a_norm, "w_q_b": w_q_b,
        "kv_a_norm": kv_a_norm, "w_kv_b": w_kv_b, "q_norm": q_norm, "k_norm": k_norm,
        "shift_prev": shift_prev, "shift_next": shift_next,
        "decay_w0_fwd": decay_w0_fwd, "decay_w2_fwd": decay_w2_fwd,
        "decay_w0_bwd": decay_w0_bwd, "decay_w2_bwd": decay_w2_bwd,
        "icl_a0_fwd": icl_a0_fwd, "icl_a2_fwd": icl_a2_fwd,
        "icl_a0_bwd": icl_a0_bwd, "icl_a2_bwd": icl_a2_bwd,
        "gate_w2": gate_w2, "k_k": k_k, "k_a": k_a, "r_k": r_k, "ln_x_w": ln_x_w, "ln_x_b": ln_x_b,
        "w_branch_mla": w_branch_mla, "w_branch_rwkv": w_branch_rwkv, "w_out": w_out,
        "ffn_norm": ffn_norm, "w_router_group": w_router_group, "b_router_group": b_router_group,
        "w_router_expert": w_router_expert, "b_router_expert": b_router_expert,
        "w_expert_gate": w_expert_gate, "w_expert_up": w_expert_up, "w_expert_down": w_expert_down,
    }
    wts = _prepare_weights({name: val[0] for name, val in p.items()})
    return _trunk(x_prompt, meta_tokens, wts), _trunk(x_sample, meta_tokens, wts)
```

```python
import functools
import math

import jax
import jax.numpy as jnp
from jax import lax
from jax.experimental import pallas as pl
from jax.experimental.pallas import tpu as pltpu
from jax.experimental.pallas import tpu_sc as plsc

F32 = jnp.float32
BF16 = jnp.bfloat16

N_META = 16
RMS_EPS = 1e-6
MLA_HEADS = 8
QK_NOPE = 128
QK_ROPE = 64
QK_HEAD = QK_NOPE + QK_ROPE
V_HEAD = 128
V_WIDE = 256
Q_LORA = 512
KV_LORA = 512
ROPE_THETA = 10000.0
RWKV_HEADS = 16
RWKV_HEAD = 64
RWKV_WIDTH = RWKV_HEADS * RWKV_HEAD
GN_EPS = 64e-5
N_GROUPS = 4
EXPERTS_PER_GROUP = 8
N_EXPERTS = N_GROUPS * EXPERTS_PER_GROUP
D_EXPERT = 512

Z_RW = 3 * RWKV_WIDTH + 4 * 64 + 128
Z_MLA = 1152
Z_GATE = Z_RW + Z_MLA
Z_TILE = 2176

LANES = 128
CHUNK = 64
META_ROWS = 128
ROW_BLOCK = 256
SC_WIN = 128
SC_WORDS = 256
NEG_BIG = -1e30
MIB = 1024 * 1024


def _params(sem, vmem_mib):
    return pltpu.CompilerParams(dimension_semantics=sem, vmem_limit_bytes=vmem_mib * MIB)


def _dot(a, b):
    return jnp.dot(a, b, preferred_element_type=F32)


def _dot_nt(a, b):
    return lax.dot_general(a, b, (((1,), (1,)), ((), ())), preferred_element_type=F32)


def _dot_tn(a, b):
    return lax.dot_general(a, b, (((0,), (0,)), ((), ())), preferred_element_type=F32)


def _split2(x):
    hi = x.astype(BF16)
    lo = (x - hi.astype(F32)).astype(BF16)
    return hi, lo


def _dot_exact_lhs(a_bf16, x):
    hi, lo = _split2(x)
    return _dot(a_bf16, hi) + _dot(a_bf16, lo)


def _dot_wide(x, w):
    x1, x2 = _split2(x)
    w1, w2 = _split2(w)
    return _dot(x1, w1) + (_dot(x1, w2) + _dot(x2, w1))


def _norm_matmul_kernel(x_ref, g_ref, w_ref, o_ref, nb_ref):
    @pl.when(pl.program_id(1) == 0)
    def _():
        x = x_ref[...]
        y = x * lax.rsqrt(jnp.mean(x * x, axis=-1, keepdims=True) + RMS_EPS)
        nb_ref[...] = (y * g_ref[...]).astype(BF16)

    o_ref[...] = _dot(nb_ref[...], w_ref[...])


def _norm_matmul(x, gain, w, tn, tm=512):
    m, d = x.shape
    n = w.shape[1]
    tm = min(tm, m)
    assert m % tm == 0 and n % tn == 0
    return pl.pallas_call(
        _norm_matmul_kernel,
        out_shape=jax.ShapeDtypeStruct((m, n), F32),
        grid=(m // tm, n // tn),
        in_specs=[
            pl.BlockSpec((tm, d), lambda i, j: (i, 0)),
            pl.BlockSpec((1, d), lambda i, j: (0, 0)),
            pl.BlockSpec((d, tn), lambda i, j: (0, j)),
        ],
        out_specs=pl.BlockSpec((tm, tn), lambda i, j: (i, j)),
        scratch_shapes=[pltpu.VMEM((tm, d), BF16)],
        compiler_params=_params(("parallel", "arbitrary"), 48),
        name="norm_matmul",
    )(x, gain.reshape(1, d).astype(F32), w)


def _mla_prep_kernel(z_ref, cos_ref, sin_ref, qa_ref, kva_ref, wq_ref, wkv_ref, qn_ref, kn_ref,
                     q_ref, k_ref, v_ref):
    z = z_ref[...]
    cq = z[:, :Q_LORA]
    ckv = z[:, Q_LORA:Q_LORA + KV_LORA]
    kr = z[:, Q_LORA + KV_LORA:Q_LORA + KV_LORA + QK_ROPE]
    cos = cos_ref[...]
    sin = sin_ref[...]

    def rms(x, g):
        return x * lax.rsqrt(jnp.mean(x * x, axis=-1, keepdims=True) + RMS_EPS) * g

    def rope(x):
        swapped = jnp.concatenate([x[:, QK_ROPE // 2:], x[:, :QK_ROPE // 2]], axis=-1)
        return x * cos + swapped * sin

    q_all = _dot(rms(cq, qa_ref[...]).astype(BF16), wq_ref[...])
    kv_all = _dot(rms(ckv, kva_ref[...]).astype(BF16), wkv_ref[...])
    qn = qn_ref[...]
    kn = kn_ref[...]
    scale = QK_HEAD ** -0.5 * math.log2(math.e)
    ones_col = (lax.broadcasted_iota(jnp.int32, (z.shape[0], V_WIDE - V_HEAD), 1) == 0).astype(BF16)
    kr_sq = jnp.sum(kr * kr, axis=-1, keepdims=True)
    kr_rot = rope(kr * kn[:, QK_NOPE:])
    for h in range(MLA_HEADS):
        qh = q_all[:, h * QK_HEAD:(h + 1) * QK_HEAD]
        qh = qh * (lax.rsqrt(jnp.mean(qh * qh, axis=-1, keepdims=True) + RMS_EPS) * scale) * qn
        q_ref[0, h, :, :QK_NOPE] = qh[:, :QK_NOPE].astype(BF16)
        q_ref[0, h, :, QK_NOPE:] = rope(qh[:, QK_NOPE:]).astype(BF16)
        base = h * (QK_NOPE + V_HEAD)
        k_nope = kv_all[:, base:base + QK_NOPE]
        ms = (jnp.sum(k_nope * k_nope, axis=-1, keepdims=True) + kr_sq) * (1.0 / QK_HEAD)
        inv = lax.rsqrt(ms + RMS_EPS)
        k_ref[0, h, :, :QK_NOPE] = (k_nope * inv * kn[:, :QK_NOPE]).astype(BF16)
        k_ref[0, h, :, QK_NOPE:] = (kr_rot * inv).astype(BF16)
        v_ref[0, h, :, :V_HEAD] = kv_all[:, base + QK_NOPE:base + QK_NOPE + V_HEAD].astype(BF16)
        v_ref[0, h, :, V_HEAD:] = ones_col


def _mla_prep(z, batch, seq, cos, sin, qa, kva, wq, wkv, qn, kn, tm=512):
    tm = min(tm, seq)
    assert seq % tm == 0 and Z_RW % Z_MLA == 0
    nt = seq // tm
    full = lambda b, i: (0, 0)
    return pl.pallas_call(
        _mla_prep_kernel,
        out_shape=(
            jax.ShapeDtypeStruct((batch, MLA_HEADS, seq, QK_HEAD), BF16),
            jax.ShapeDtypeStruct((batch, MLA_HEADS, seq, QK_HEAD), BF16),
            jax.ShapeDtypeStruct((batch, MLA_HEADS, seq, V_WIDE), BF16),
        ),
        grid=(batch, nt),
        in_specs=[
            pl.BlockSpec((tm, Z_MLA), lambda b, i: (b * nt + i, Z_RW // Z_MLA)),
            pl.BlockSpec((tm, QK_ROPE), lambda b, i: (i, 0)),
            pl.BlockSpec((tm, QK_ROPE), lambda b, i: (i, 0)),
            pl.BlockSpec((1, Q_LORA), full),
            pl.BlockSpec((1, KV_LORA), full),
            pl.BlockSpec(wq.shape, full),
            pl.BlockSpec(wkv.shape, full),
            pl.BlockSpec((1, QK_HEAD), full),
            pl.BlockSpec((1, QK_HEAD), full),
        ],
        out_specs=(
            pl.BlockSpec((1, MLA_HEADS, tm, QK_HEAD), lambda b, i: (b, 0, i, 0)),
            pl.BlockSpec((1, MLA_HEADS, tm, QK_HEAD), lambda b, i: (b, 0, i, 0)),
            pl.BlockSpec((1, MLA_HEADS, tm, V_WIDE), lambda b, i: (b, 0, i, 0)),
        ),
        compiler_params=_params(("parallel", "parallel"), 48),
        name="mla_prep",
    )(z, cos, sin, qa, kva, wq, wkv, qn, kn)


def _attention_kernel(q_ref, k_ref, v_ref, km_ref, vm_ref, o_ref, *, tk, n_split):
    q = q_ref[0, 0]
    tq = q.shape[0]
    seq = k_ref.shape[2]

    rows = tq // n_split
    qs = [q[g * rows:(g + 1) * rows] for g in range(n_split)]
    ss = [_dot_nt(x, km_ref[0]) for x in qs]
    lane = lax.broadcasted_iota(jnp.int32, ss[0].shape, 1)
    ss = [jnp.where(lane < N_META, s, NEG_BIG) for s in ss]
    ms = [jnp.max(s, axis=-1, keepdims=True) for s in ss]
    accs = [_dot(jnp.exp2(s - m).astype(BF16), vm_ref[0]) for s, m in zip(ss, ms)]

    def body(c, carry):
        ms, accs = carry
        off = pl.multiple_of(c * tk, tk)
        k = k_ref[0, 0, pl.ds(off, tk), :]
        v = v_ref[0, 0, pl.ds(off, tk), :]
        ss = [_dot_nt(x, k) for x in qs]
        m_new = [jnp.maximum(m, jnp.max(s, axis=-1, keepdims=True)) for m, s in zip(ms, ss)]
        ps = [jnp.exp2(s - m).astype(BF16) for s, m in zip(ss, m_new)]
        accs = [jnp.exp2(m - mn) * a + _dot(p, v) for m, mn, a, p in zip(ms, m_new, accs, ps)]
        return m_new, accs

    ms, accs = lax.fori_loop(0, seq // tk, body, (ms, accs))
    for g in range(n_split):
        acc = accs[g]
        o_ref[0, g * rows:(g + 1) * rows, :] = (
            acc[:, :V_HEAD] / acc[:, V_HEAD:V_HEAD + 1]).astype(BF16)


def _attention(q, k, v, k_meta, v_meta, tq=1024, tk=2048, n_split=4):
    batch, heads, seq, _ = q.shape
    tq = min(tq, seq)
    tk = min(tk, seq)
    assert seq % tq == 0 and seq % tk == 0 and tq % (16 * n_split) == 0
    return pl.pallas_call(
        functools.partial(_attention_kernel, tk=tk, n_split=n_split),
        out_shape=jax.ShapeDtypeStruct((batch, seq, heads * V_HEAD), BF16),
        grid=(batch, heads, seq // tq),
        in_specs=[
            pl.BlockSpec((1, 1, tq, QK_HEAD), lambda b, h, i: (b, h, i, 0)),
            pl.BlockSpec((1, 1, seq, QK_HEAD), lambda b, h, i: (b, h, 0, 0)),
            pl.BlockSpec((1, 1, seq, V_WIDE), lambda b, h, i: (b, h, 0, 0)),
            pl.BlockSpec((1, META_ROWS, QK_HEAD), lambda b, h, i: (h, 0, 0)),
            pl.BlockSpec((1, META_ROWS, V_WIDE), lambda b, h, i: (h, 0, 0)),
        ],
        out_specs=pl.BlockSpec((1, tq, V_HEAD), lambda b, h, i: (b, i, h)),
        compiler_params=_params(("parallel", "parallel", "arbitrary"), 48),
        name="mla_attention",
    )(q, k, v, k_meta, v_meta)


def _head_sum(x, ones_blk):
    xb = x.astype(BF16)
    cols = [_dot(xb[:, c * LANES:(c + 1) * LANES], ones_blk) for c in range(x.shape[1] // LANES)]
    return jnp.concatenate(cols, axis=-1)


def _block_diag(x, bd_mask):
    return jnp.where(bd_mask, jnp.concatenate([x, x], axis=0), jnp.zeros((), x.dtype))


def _tri_inverse(a_list, row, col, bd_mask):
    def same_block(n):
        return (row // n) == (col // n)

    eye = (row == col).astype(F32)
    apow = [jnp.where(same_block(16), a, 0.0) for a in a_list]
    t = [eye - a for a in apow]
    for _ in range(3):
        apow_b = [a.astype(BF16) for a in apow]
        apow = [_dot(a, _block_diag(a, bd_mask)) for a in apow_b]
        t = [ti + _dot(ti.astype(BF16), _block_diag(a.astype(BF16), bd_mask))
             for ti, a in zip(t, apow)]
    n = 32
    while n <= CHUNK:
        off_mask = same_block(n) & jnp.logical_not(same_block(n // 2))
        t_bd = [_block_diag(ti.astype(BF16), bd_mask) for ti in t]
        inner = [_dot(jnp.where(off_mask, a, 0.0).astype(BF16), tb).astype(BF16)
                 for a, tb in zip(a_list, t_bd)]
        t = [ti - _dot(ti.astype(BF16), _block_diag(x, bd_mask)) for ti, x in zip(t, inner)]
        n *= 2
    return t


def _rwkv_kernel(*refs, reverse, n_lead, nb):
    if reverse:
        (zs_ref, kkn_ref, yf_ref, bf_ref, w0_ref, w2_ref, a0_ref, a2_ref, ka_ref, rk_ref,
         g2_ref, lnw_ref, lnb_ref, o_ref, st_ref) = refs
    else:
        (z_ref, zp_ref, zn_ref, lead_ref, sp_ref, sn_ref, w0_ref, w2_ref, a0_ref, a2_ref,
         kk_ref, ka_ref, rk_ref, y_ref, bon_ref, zs_ref, kkn_ref, st_ref) = refs
    c = pl.program_id(1)
    nc = pl.num_programs(1)
    phys = (nc - c) if reverse else c

    @pl.when(c == 0)
    def _():
        st_ref[...] = jnp.zeros_like(st_ref)

    w = RWKV_WIDTH
    rows = lax.broadcasted_iota(jnp.int32, (CHUNK, 1), 0)
    lane_r = lax.broadcasted_iota(jnp.int32, (LANES, LANES), 0)
    lane_c = lax.broadcasted_iota(jnp.int32, (LANES, LANES), 1)
    bd_mask = (lane_r // RWKV_HEAD) == (lane_c // RWKV_HEAD)
    ones_blk = bd_mask.astype(BF16)
    row = lax.broadcasted_iota(jnp.int32, (CHUNK, LANES), 0)
    lane = lax.broadcasted_iota(jnp.int32, (CHUNK, LANES), 1)
    col = lane % CHUNK
    first_head = lane < RWKV_HEAD
    if reverse:
        incl = col >= row
        strict = col > row
    else:
        incl = col <= row
        strict = col < row
    tri = incl[:, :CHUNK].astype(BF16)

    def operands(i):
        if reverse:
            zs = zs_ref[i]
            kkn = kkn_ref[i]
            r, k, v = zs[:, :w], zs[:, w:2 * w], zs[:, 2 * w:3 * w]
        else:
            z = jnp.where(phys == 0, lead_ref[...], z_ref[i])
            prev_row = jnp.where(phys == 1, lead_ref[CHUNK - 1:CHUNK, :], zp_ref[i, 7:8, :])
            prev_row = jnp.where(phys > 0, prev_row, 0.0)
            next_row = jnp.where(phys < nc - 1, zn_ref[i, 0:1, :], 0.0)
            z_prev = jnp.where(rows == 0, prev_row, pltpu.roll(z, 1, axis=0))
            z_next = jnp.where(rows == CHUNK - 1, next_row, pltpu.roll(z, CHUNK - 1, axis=0))
            zs = z + sp_ref[...] * (z_prev - z) + sn_ref[...] * (z_next - z)
            zs_ref[i] = zs
            live = jnp.logical_or(phys > 0, rows >= n_lead).astype(F32)
            r = zs[:, :w] * live
            k = zs[:, w:2 * w] * live
            v = zs[:, 2 * w:3 * w] * live
            kk = k * kk_ref[...]
            kkn = kk * jnp.minimum(lax.rsqrt(_head_sum(kk * kk, ones_blk)), 1e12)
            kkn_ref[i] = kkn
        wd = zs[:, 3 * w:3 * w + LANES]
        ad = zs[:, 3 * w + LANES:3 * w + 2 * LANES]
        w_raw = w0_ref[...] + _dot(jnp.tanh(wd).astype(BF16), w2_ref[...])
        lw = (-math.exp(-0.5)) * jax.nn.sigmoid(w_raw)
        icl = jax.nn.sigmoid(a0_ref[...] + _dot(ad.astype(BF16), a2_ref[...]))
        k_rep = k * (1.0 + (icl - 1.0) * ka_ref[...])
        bvec = kkn * icl
        bonus = _head_sum(r * k_rep * rk_ref[...], ones_blk) * v
        l_in = _dot_exact_lhs(tri, lw)
        l_tot = l_in[0:1, :] if reverse else l_in[CHUNK - 1:CHUNK, :]
        e_neg = jnp.exp(-l_in)
        e_rem = jnp.exp(l_tot - l_in)
        ops = {
            "kd": (kkn * jnp.exp(l_in - lw)).astype(BF16), "rd": (r * jnp.exp(l_in)).astype(BF16),
            "bn": (bvec * e_neg).astype(BF16), "kn": (k_rep * e_neg).astype(BF16),
            "bh": (bvec * e_rem).astype(BF16), "kh": (k_rep * e_rem).astype(BF16),
            "v": v.astype(BF16),
        }
        p_tot_t = jnp.transpose(jnp.broadcast_to(jnp.exp(l_tot), (8, w)))
        return ops, p_tot_t, bonus, zs

    per_seq = [operands(i) for i in range(nb)]

    n_pairs = RWKV_HEADS // 2
    items = [(i, p) for i in range(nb) for p in range(n_pairs)]
    bd = lambda x: _block_diag(x, bd_mask)
    diag_blocks = lambda x: jnp.where(first_head, x[:RWKV_HEAD], x[RWKV_HEAD:])
    kd_p, rd_p, bn_p, kn_p, bh_p, kh_p, v_p = (
        [per_seq[i][0][name][:, p * LANES:(p + 1) * LANES] for i, p in items]
        for name in ("kd", "rd", "bn", "kn", "bh", "kh", "v"))
    v_bd = [bd(x) for x in v_p]
    kr_p = [jnp.concatenate([a, b], axis=0) for a, b in zip(kd_p, rd_p)]
    prod_b = [_dot_nt(x, bd(y)) for x, y in zip(kr_p, bn_p)]
    prod_k = [_dot_nt(x, bd(y)) for x, y in zip(kr_p, kn_p)]
    a_ab = [jnp.where(strict, x[:CHUNK], 0.0) for x in prod_b]
    m_rb = [jnp.where(incl, x[CHUNK:], 0.0).astype(BF16) for x in prod_b]
    a_ak = [jnp.where(strict, x[:CHUNK], 0.0).astype(BF16) for x in prod_k]
    m_rk = [jnp.where(incl, x[CHUNK:], 0.0).astype(BF16) for x in prod_k]
    akv = [_dot(a, vv).astype(BF16) for a, vv in zip(a_ak, v_bd)]
    t_inv = [t.astype(BF16) for t in _tri_inverse(a_ab, row, col, bd_mask)]
    wmat = [(-_dot(t, bd(x))).astype(BF16) for t, x in zip(t_inv, kd_p)]
    u0 = [(-_dot(t, bd(x))).astype(BF16) for t, x in zip(t_inv, akv)]
    u0_bd = [bd(x) for x in u0]
    st = [st_ref[i * n_pairs + p] for i, p in items]
    st_bd = [bd(s.astype(BF16)) for s in st]
    q_mat = [(x.astype(F32) + _dot(m, bd(wm))).astype(BF16)
             for x, m, wm in zip(rd_p, m_rb, wmat)]
    ys = [_dot(qm, sb) + _dot(mb, u) + _dot(mk, vv)
          for qm, sb, mb, u, mk, vv in zip(q_mat, st_bd, m_rb, u0_bd, m_rk, v_bd)]
    g_mat = [diag_blocks(_dot_tn(b, wm)).astype(BF16) for b, wm in zip(bh_p, wmat)]
    h_mat = [diag_blocks(_dot_tn(b, u) + _dot_tn(kx, vv))
             for b, u, kx, vv in zip(bh_p, u0, kh_p, v_p)]
    for n, (i, p) in enumerate(items):
        p_tot_t = per_seq[i][1]
        decay = jnp.where(first_head, p_tot_t[2 * p * RWKV_HEAD:(2 * p + 1) * RWKV_HEAD, 0:1],
                          p_tot_t[(2 * p + 1) * RWKV_HEAD:(2 * p + 2) * RWKV_HEAD, 0:1])
        st_ref[i * n_pairs + p] = decay * st[n] + _dot(g_mat[n], st_bd[n]) + h_mat[n]

    for i in range(nb):
        y = jnp.concatenate(ys[i * n_pairs:(i + 1) * n_pairs], axis=-1)
        _, _, bonus, zs = per_seq[i]
        if not reverse:
            y_ref[i] = y
            bon_ref[i] = bonus
        else:
            y = y + yf_ref[i]
            mu = _head_sum(y, ones_blk) * (1.0 / RWKV_HEAD)
            yc = y - mu
            var = _head_sum(yc * yc, ones_blk) * (1.0 / RWKV_HEAD)
            yn = (yc * lax.rsqrt(var + GN_EPS) * lnw_ref[...] + lnb_ref[...]
                  + (bonus + bf_ref[i]))
            gd = zs[:, 3 * w + 2 * LANES:3 * w + 3 * LANES]
            g = _dot(jax.nn.sigmoid(gd).astype(BF16), g2_ref[...])
            o_ref[i] = (yn * g).astype(BF16)


def _rwkv(z_rw, lead, shared, fwd_p, bwd_p, tail, nb=2):
    batch, seq, _ = z_rw.shape
    zc = Z_RW
    nb = min(nb, batch)
    assert batch % nb == 0 and seq % CHUNK == 0
    ncx = seq // CHUNK
    lp = seq + CHUNK
    sub = CHUNK // 8
    w = RWKV_WIDTH
    n_lead = CHUNK - N_META
    full = lambda b, c: (0, 0)
    row_w = pl.BlockSpec((1, w), full)
    mat_w = pl.BlockSpec((LANES, w), full)
    state = pltpu.VMEM((nb * RWKV_HEADS // 2, RWKV_HEAD, 2 * RWKV_HEAD), F32)

    chunk = lambda b, c: (b, c, 0)
    fwd_specs = [
        pl.BlockSpec((nb, CHUNK, zc), lambda b, c: (b, jnp.maximum(c - 1, 0), 0)),
        pl.BlockSpec((nb, 8, zc), lambda b, c: (b, jnp.maximum((c - 1) * sub - 1, 0), 0)),
        pl.BlockSpec((nb, 8, zc), lambda b, c: (b, jnp.minimum(c * sub, ncx * sub - 1), 0)),
        pl.BlockSpec((CHUNK, zc), full),
        pl.BlockSpec((1, zc), full), pl.BlockSpec((1, zc), full),
        row_w, mat_w, row_w, mat_w, row_w, row_w, row_w,
    ]
    fwd_args = [z_rw, z_rw, z_rw, lead, shared["sp"], shared["sn"], fwd_p["w0"], fwd_p["w2"],
                fwd_p["a0"], fwd_p["a2"], shared["k_k"], shared["k_a"], shared["r_k"]]
    y_f, bon_f, zs, kkn = pl.pallas_call(
        functools.partial(_rwkv_kernel, reverse=False, n_lead=n_lead, nb=nb),
        out_shape=(jax.ShapeDtypeStruct((batch, lp, w), F32),
                   jax.ShapeDtypeStruct((batch, lp, w), F32),
                   jax.ShapeDtypeStruct((batch, lp, zc), F32),
                   jax.ShapeDtypeStruct((batch, lp, w), F32)),
        grid=(batch // nb, ncx + 1),
        in_specs=fwd_specs,
        out_specs=(pl.BlockSpec((nb, CHUNK, w), chunk), pl.BlockSpec((nb, CHUNK, w), chunk),
                   pl.BlockSpec((nb, CHUNK, zc), chunk), pl.BlockSpec((nb, CHUNK, w), chunk)),
        scratch_shapes=[state],
        compiler_params=_params(("parallel", "arbitrary"), 48),
        name="rwkv7_fwd",
    )(*fwd_args)

    rchunk = lambda b, c: (b, ncx - c, 0)
    bwd_specs = [
        pl.BlockSpec((nb, CHUNK, zc), rchunk), pl.BlockSpec((nb, CHUNK, w), rchunk),
        pl.BlockSpec((nb, CHUNK, w), rchunk), pl.BlockSpec((nb, CHUNK, w), rchunk),
        row_w, mat_w, row_w, mat_w, row_w, row_w, mat_w, row_w, row_w,
    ]
    bwd_args = [zs, kkn, y_f, bon_f, bwd_p["w0"], bwd_p["w2"], bwd_p["a0"], bwd_p["a2"],
                shared["k_a"], shared["r_k"], tail["g2"], tail["ln_w"], tail["ln_b"]]
    return pl.pallas_call(
        functools.partial(_rwkv_kernel, reverse=True, n_lead=n_lead, nb=nb),
        out_shape=jax.ShapeDtypeStruct((batch, seq, w), BF16),
        grid=(batch // nb, ncx),
        in_specs=bwd_specs,
        out_specs=pl.BlockSpec((nb, CHUNK, w), lambda b, c: (b, ncx - 1 - c, 0)),
        scratch_shapes=[state],
        compiler_params=_params(("parallel", "arbitrary"), 48),
        name="rwkv7_bwd",
    )(*bwd_args)


def _merge_kernel(oa_ref, ob_ref, ga_ref, gb_ref, wa_ref, wb_ref, o_ref):
    a = _dot(oa_ref[...], wa_ref[...])
    b = _dot(ob_ref[...], wb_ref[...])
    o_ref[...] = (jax.nn.sigmoid(ga_ref[...]) * a + jax.nn.sigmoid(gb_ref[...]) * b).astype(BF16)


def _merge(o_a, o_b, z, w_a, w_b, tm=512, tn=512):
    m, d = o_a.shape[0], w_a.shape[1]
    tm = min(tm, m)
    assert Z_GATE % tn == 0 and d % tn == 0
    nj = d // tn
    g0 = Z_GATE // tn
    return pl.pallas_call(
        _merge_kernel,
        out_shape=jax.ShapeDtypeStruct((m, d), BF16),
        grid=(m // tm, nj),
        in_specs=[
            pl.BlockSpec((tm, o_a.shape[1]), lambda i, j: (i, 0)),
            pl.BlockSpec((tm, o_b.shape[1]), lambda i, j: (i, 0)),
            pl.BlockSpec((tm, tn), lambda i, j: (i, g0 + j)),
            pl.BlockSpec((tm, tn), lambda i, j: (i, g0 + nj + j)),
            pl.BlockSpec((w_a.shape[0], tn), lambda i, j: (0, j)),
            pl.BlockSpec((w_b.shape[0], tn), lambda i, j: (0, j)),
        ],
        out_specs=pl.BlockSpec((tm, tn), lambda i, j: (i, j)),
        compiler_params=_params(("parallel", "parallel"), 48),
        name="branch_merge",
    )(o_a, o_b, z, z, w_a, w_b)


def _pack_rows(x):
    half = x.shape[1] // 2
    bits = lax.bitcast_convert_type(x.astype(BF16).astype(F32), jnp.uint32)
    words = bits[:, :half] | (bits[:, half:] >> 16)
    return [words[:, s * SC_WORDS:(s + 1) * SC_WORDS] for s in range(half // SC_WORDS)]


def _unpack_rows(slabs):
    words = jnp.concatenate(slabs, axis=1)
    hi = lax.bitcast_convert_type(words & jnp.uint32(0xFFFF0000), F32)
    lo = lax.bitcast_convert_type(words << 16, F32)
    return jnp.concatenate([hi, lo], axis=1)


def _out_router_kernel(x_ref, mg_ref, wo_ref, g_ref, wr_ref, br_ref, h_ref, n2_ref, cmb_ref):
    h = x_ref[...] + _dot(mg_ref[...], wo_ref[...])
    h_ref[...] = h
    n2 = h * lax.rsqrt(jnp.mean(h * h, axis=-1, keepdims=True) + RMS_EPS) * g_ref[...]
    for s, slab in enumerate(_pack_rows(n2)):
        n2_ref[s] = slab
    logits = _dot_wide(n2, wr_ref[...]) + br_ref[...]
    lane = lax.broadcasted_iota(jnp.int32, logits.shape, 1)
    big = jnp.int32(1 << 20)
    g_mask = (lane >= N_EXPERTS) & (lane < N_EXPERTS + N_GROUPS)
    gl = jnp.where(g_mask, logits, NEG_BIG)
    g_max = jnp.max(gl, axis=-1, keepdims=True)
    pg_top = 1.0 / jnp.sum(jnp.where(g_mask, jnp.exp(gl - g_max), 0.0), axis=-1, keepdims=True)
    gi = jnp.min(jnp.where(gl == g_max, lane, big), axis=-1, keepdims=True) - N_EXPERTS
    e_mask = (lane >= gi * EXPERTS_PER_GROUP) & (lane < (gi + 1) * EXPERTS_PER_GROUP)
    el = jnp.where(e_mask, logits, NEG_BIG)
    m1 = jnp.max(el, axis=-1, keepdims=True)
    i1 = jnp.min(jnp.where(el == m1, lane, big), axis=-1, keepdims=True)
    el2 = jnp.where(lane == i1, NEG_BIG, el)
    m2 = jnp.max(el2, axis=-1, keepdims=True)
    i2 = jnp.min(jnp.where(el2 == m2, lane, big), axis=-1, keepdims=True)
    e21 = jnp.exp(m2 - m1)
    gate1 = pg_top / (1.0 + e21)
    gate2 = pg_top * e21 / (1.0 + e21)
    cmb_ref[...] = jnp.where(lane == i1, gate1, 0.0) + jnp.where(lane == i2, gate2, 0.0)


def _out_router(x, merged, w_out, gain, w_r, b_r, tm=256):
    m, d = x.shape
    tm = min(tm, m)
    slabs = d // 2 // SC_WORDS
    full = lambda i: (0, 0)
    row = lambda i: (i, 0)
    return pl.pallas_call(
        _out_router_kernel,
        out_shape=(jax.ShapeDtypeStruct((m, d), F32),
                   jax.ShapeDtypeStruct((slabs, m, SC_WORDS), jnp.uint32),
                   jax.ShapeDtypeStruct((m, LANES), F32)),
        grid=(m // tm,),
        in_specs=[
            pl.BlockSpec((tm, d), row), pl.BlockSpec((tm, d), row),
            pl.BlockSpec((d, d), full), pl.BlockSpec((1, d), full),
            pl.BlockSpec((d, LANES), full), pl.BlockSpec((1, LANES), full),
        ],
        out_specs=(pl.BlockSpec((tm, d), row),
                   pl.BlockSpec((slabs, tm, SC_WORDS), lambda i: (0, i, 0)),
                   pl.BlockSpec((tm, LANES), row)),
        compiler_params=_params(("parallel",), 48),
        name="out_proj_router",
    )(x, merged, w_out, gain, w_r, b_r)


def _expert_count_kernel(cmb_ref, cnt_ref):
    @pl.when(pl.program_id(0) == 0)
    def _():
        cnt_ref[...] = jnp.zeros_like(cnt_ref)

    sel = (cmb_ref[...] > 0.0).astype(F32)
    cnt_ref[...] += jnp.sum(sel.reshape(-1, 8, LANES), axis=0)


def _expert_counts(cmb, tm=2048):
    m = cmb.shape[0]
    tm = min(tm, m)
    return pl.pallas_call(
        _expert_count_kernel,
        out_shape=jax.ShapeDtypeStruct((8, LANES), F32),
        grid=(m // tm,),
        in_specs=[pl.BlockSpec((tm, LANES), lambda i: (i, 0))],
        out_specs=pl.BlockSpec((8, LANES), lambda i: (0, 0)),
        compiler_params=_params(("arbitrary",), 32),
        name="moe_expert_counts",
    )(cmb)


def _dispatch_index_kernel(cmb_ref, tri_ref, tril_ref, off_ref, dest_ref, gw_ref, base_ref, *, spare):
    @pl.when(pl.program_id(0) == 0)
    def _():
        base_ref[...] = jnp.zeros_like(base_ref)

    tm = cmb_ref.shape[0]
    seg = tri_ref.shape[0]
    base = base_ref[:, 0:1]
    which = lax.broadcasted_iota(jnp.int32, (LANES, seg), 0)
    for s in range(tm // seg):
        cmb_t = jnp.transpose(cmb_ref[s * seg:(s + 1) * seg, :])
        sel = cmb_t > 0.0
        sel_b = sel.astype(BF16)
        pos = _dot(sel_b, tri_ref[...]) + base
        base = pos[:, seg - 1:seg]
        nth = _dot(tril_ref[...], sel_b)
        row = off_ref[...] + pos - 1.0
        gws = []
        for k in range(2):
            pick = sel & (nth == float(k + 1))
            found = jnp.sum(jnp.where(pick, 1.0, 0.0), axis=0, keepdims=True) > 0.0
            dst = jnp.sum(jnp.where(pick, row, 0.0), axis=0, keepdims=True)
            dest_ref[k:k + 1, s * seg:(s + 1) * seg] = jnp.where(found, dst, spare).astype(jnp.int32)
            gws.append(jnp.sum(jnp.where(pick, cmb_t, 0.0), axis=0, keepdims=True))
        gmat = jnp.where(which == 0, gws[0], jnp.where(which == 1, gws[1], 0.0))
        gw_ref[s * seg:(s + 1) * seg, :] = jnp.transpose(gmat)
    base_ref[...] = jnp.broadcast_to(base, base_ref.shape)


def _dispatch_index(cmb, off_col, spare, tm=2048, seg=1024):
    m = cmb.shape[0]
    tm = min(tm, m)
    seg = min(seg, tm)
    idx = jnp.arange(seg)
    tri = (idx[:, None] <= idx[None, :]).astype(BF16)
    lid = jnp.arange(LANES)
    tril = (lid[:, None] >= lid[None, :]).astype(BF16)
    return pl.pallas_call(
        functools.partial(_dispatch_index_kernel, spare=float(spare)),
        out_shape=(jax.ShapeDtypeStruct((2, m), jnp.int32), jax.ShapeDtypeStruct((m, LANES), F32)),
        grid=(m // tm,),
        in_specs=[pl.BlockSpec((tm, LANES), lambda i: (i, 0)),
                  pl.BlockSpec((seg, seg), lambda i: (0, 0)),
                  pl.BlockSpec((LANES, LANES), lambda i: (0, 0)),
                  pl.BlockSpec((LANES, 1), lambda i: (0, 0))],
        out_specs=(pl.BlockSpec((2, tm), lambda i: (0, i)), pl.BlockSpec((tm, LANES), lambda i: (i, 0))),
        scratch_shapes=[pltpu.VMEM((LANES, LANES), F32)],
        compiler_params=_params(("arbitrary",), 32),
        name="moe_dispatch_index",
    )(cmb, tri, tril, off_col)


def _sc_scatter_rows(src, idx, n_dst):
    n_src, words = src.shape
    n_idx = idx.shape[0]
    n_win = n_src // SC_WIN
    mesh = plsc.VectorSubcoreMesh(core_axis_name="c", subcore_axis_name="s")

    @pl.kernel(out_type=jax.ShapeDtypeStruct((n_dst, words), src.dtype), mesh=mesh,
               scratch_types=[])
    def scatter(x_hbm, i_hbm, o_hbm):
        def body(x_vmem, i_vmem):
            pltpu.sync_copy(x_vmem, o_hbm.at[i_vmem.at[0]])

        pltpu.emit_pipeline(
            body, grid=(n_idx // SC_WIN,),
            in_specs=[pl.BlockSpec((SC_WIN, words), index_map=lambda i: (i % n_win, 0)),
                      pl.BlockSpec((1, SC_WIN), index_map=lambda i: (0, i))],
            out_specs=[],
            core_axis_name=("c", "s"), dimension_semantics=(pltpu.PARALLEL,),
        )(x_hbm, i_hbm)

    return scatter(src, idx.reshape(1, n_idx))


def _sc_gather_rows(src, idx):
    words = src.shape[1]
    n_idx = idx.shape[0]
    mesh = plsc.VectorSubcoreMesh(core_axis_name="c", subcore_axis_name="s")

    @pl.kernel(out_type=jax.ShapeDtypeStruct((n_idx, words), src.dtype), mesh=mesh)
    def gather(x_hbm, i_hbm, o_hbm):
        def body(i_vmem, o_vmem):
            pltpu.sync_copy(x_hbm.at[i_vmem.at[0]], o_vmem)

        pltpu.emit_pipeline(
            body, grid=(n_idx // SC_WIN,),
            in_specs=[pl.BlockSpec((1, SC_WIN), index_map=lambda i: (0, i))],
            out_specs=[pl.BlockSpec((SC_WIN, words), index_map=lambda i: (i, 0))],
            core_axis_name=("c", "s"), dimension_semantics=(pltpu.PARALLEL,),
        )(i_hbm, o_hbm)

    return gather(src, idx.reshape(1, n_idx))


def _expert_block_kernel(blk_e_ref, n_used_ref, xs_ref, wg_ref, wu_ref, wd_ref, ys_ref):
    @pl.when(pl.program_id(0) < n_used_ref[0])
    def _():
        xe = _unpack_rows([xs_ref[s] for s in range(xs_ref.shape[0])]).astype(BF16)
        hg = _dot(xe, wg_ref[0])
        hu = _dot(xe, wu_ref[0])
        act = (hg * jax.nn.sigmoid(hg)) * hu
        for s, slab in enumerate(_pack_rows(_dot(act.astype(BF16), wd_ref[0]))):
            ys_ref[s] = slab


def _expert_blocks(blk_e, n_used, xs, w_gate, w_up, w_down):
    slabs, rows, words = xs.shape
    _, d, f = w_gate.shape
    nb = rows // ROW_BLOCK - 1
    packed = pl.BlockSpec((slabs, ROW_BLOCK, words), lambda j, be, nu: (0, j, 0))
    grid_spec = pltpu.PrefetchScalarGridSpec(
        num_scalar_prefetch=2,
        grid=(nb,),
        in_specs=[packed,
                  pl.BlockSpec((1, d, f), lambda j, be, nu: (be[j], 0, 0)),
                  pl.BlockSpec((1, d, f), lambda j, be, nu: (be[j], 0, 0)),
                  pl.BlockSpec((1, f, d), lambda j, be, nu: (be[j], 0, 0))],
        out_specs=packed,
    )
    return pl.pallas_call(
        _expert_block_kernel,
        out_shape=jax.ShapeDtypeStruct(xs.shape, xs.dtype),
        grid_spec=grid_spec,
        compiler_params=_params(("arbitrary",), 48),
        name="moe_expert_blocks",
    )(blk_e, n_used, xs, w_gate, w_up, w_down)


def _combine_kernel(h_ref, y0_ref, y1_ref, gw_ref, o_ref):
    gw = gw_ref[...]
    w0 = gw[:, 0:1]
    w1 = gw[:, 1:2]
    n_slabs = y0_ref.shape[1]
    y0 = jnp.where(w0 > 0.0, w0 * _unpack_rows([y0_ref[0, s] for s in range(n_slabs)]), 0.0)
    y1 = jnp.where(w1 > 0.0, w1 * _unpack_rows([y1_ref[0, s] for s in range(n_slabs)]), 0.0)
    o_ref[...] = h_ref[...] + (y0 + y1)


def _combine(h, y, gw, tm=512):
    m, d = h.shape
    tm = min(tm, m)
    _, slabs, _, words = y.shape
    row = lambda i: (i, 0)
    return pl.pallas_call(
        _combine_kernel,
        out_shape=jax.ShapeDtypeStruct((m, d), F32),
        grid=(m // tm,),
        in_specs=[pl.BlockSpec((tm, d), row),
                  pl.BlockSpec((1, slabs, tm, words), lambda i: (0, 0, i, 0)),
                  pl.BlockSpec((1, slabs, tm, words), lambda i: (1, 0, i, 0)),
                  pl.BlockSpec((tm, LANES), row)],
        out_specs=pl.BlockSpec((tm, d), row),
        compiler_params=_params(("parallel",), 48),
        name="moe_combine",
    )(h, y, y, gw)


def _moe_dispatch(n2p, h, cmb, w_gate, w_up, w_down):
    slabs, m, words = n2p.shape
    cnt = jnp.sum(_expert_counts(cmb), axis=0)[:N_EXPERTS].astype(jnp.int32)
    nblk = (cnt + ROW_BLOCK - 1) // ROW_BLOCK
    cum = jnp.cumsum(nblk)
    nb = 2 * m // ROW_BLOCK + N_EXPERTS
    off = ((cum - nblk) * ROW_BLOCK).astype(F32)
    off_col = jnp.zeros((LANES, 1), F32).at[:N_EXPERTS, 0].set(off)
    blk_e = jnp.sum(cum[None, :] <= jnp.arange(nb, dtype=jnp.int32)[:, None], axis=1)
    blk_e = jnp.minimum(blk_e, N_EXPERTS - 1).astype(jnp.int32)
    n_used = cum[-1:].astype(jnp.int32)
    spare = nb * ROW_BLOCK
    rows = spare + ROW_BLOCK
    dest, gw = _dispatch_index(cmb, off_col, spare)
    idx = (dest[:, None, :] + (jnp.arange(slabs, dtype=jnp.int32) * rows)[None, :, None]).reshape(-1)
    xs = _sc_scatter_rows(n2p.reshape(slabs * m, words), idx, slabs * rows)
    ys = _expert_blocks(blk_e, n_used, xs.reshape(slabs, rows, words), w_gate, w_up, w_down)
    y = _sc_gather_rows(ys.reshape(slabs * rows, words), idx)
    return _combine(h, y.reshape(2, slabs, m, words), gw)


def _prepare_weights(p):
    d = p["w_in"].shape[0]
    splits = [Q_LORA, KV_LORA, QK_ROPE, 3 * RWKV_WIDTH + 4 * 64 + 128, d, d]
    offs = [0]
    for s in splits:
        offs.append(offs[-1] + s)
    w_in = p["w_in"].astype(BF16)
    zeros64 = jnp.zeros((64, RWKV_WIDTH), F32)

    def pad_lo(a):
        return jnp.concatenate([a, zeros64], axis=0).astype(BF16)

    def pad_hi(a):
        return jnp.concatenate([zeros64, a], axis=0).astype(BF16)

    row = lambda a: a.reshape(1, -1).astype(F32)
    wr = jnp.concatenate(
        [jnp.transpose(p["w_router_expert"], (1, 0, 2)).reshape(d, N_EXPERTS), p["w_router_group"],
         jnp.zeros((d, LANES - N_EXPERTS - N_GROUPS), F32)], axis=1)
    br = jnp.concatenate([p["b_router_expert"].reshape(-1), p["b_router_group"],
                          jnp.zeros((LANES - N_EXPERTS - N_GROUPS,), F32)]).reshape(1, LANES)
    return {
        "attn_norm": p["attn_norm"],
        "w_all": jnp.concatenate(
            [w_in[:, offs[3]:offs[4]], w_in[:, offs[0]:offs[3]],
             jnp.zeros((d, Z_MLA - (offs[3] - offs[0])), BF16), w_in[:, offs[4]:offs[6]]], axis=1),
        "qa": row(p["q_a_norm"]), "kva": row(p["kv_a_norm"]),
        "wq": p["w_q_b"].astype(BF16), "wkv": p["w_kv_b"].astype(BF16),
        "qn": row(p["q_norm"]), "kn": row(p["k_norm"]),
        "shared": {"sp": row(p["shift_prev"]), "sn": row(p["shift_next"]), "k_k": row(p["k_k"]),
                   "k_a": row(p["k_a"]), "r_k": row(p["r_k"])},
        "fwd": {"w0": row(p["decay_w0_fwd"]), "w2": pad_lo(p["decay_w2_fwd"]),
                "a0": row(p["icl_a0_fwd"]), "a2": pad_lo(p["icl_a2_fwd"])},
        "bwd": {"w0": row(p["decay_w0_bwd"]), "w2": pad_hi(p["decay_w2_bwd"]),
                "a0": row(p["icl_a0_bwd"]), "a2": pad_hi(p["icl_a2_bwd"])},
        "tail": {"g2": p["gate_w2"].astype(BF16), "ln_w": row(p["ln_x_w"]), "ln_b": row(p["ln_x_b"])},
        "w_a": p["w_branch_mla"].astype(BF16), "w_b": p["w_branch_rwkv"].astype(BF16),
        "w_out": p["w_out"].astype(BF16), "ffn_norm": row(p["ffn_norm"]),
        "wr": wr, "br": br,
        "w_gate": p["w_expert_gate"].astype(BF16), "w_up": p["w_expert_up"].astype(BF16),
        "w_down": p["w_expert_down"].astype(BF16),
    }


def _rope_tables(length):
    pos = jnp.arange(length, dtype=F32)
    inv = ROPE_THETA ** (-jnp.arange(0, QK_ROPE, 2, dtype=F32) / QK_ROPE)
    ang = pos[:, None] * inv[None, :]
    cos, sin = jnp.cos(ang), jnp.sin(ang)
    return jnp.concatenate([cos, cos], axis=-1), jnp.concatenate([-sin, sin], axis=-1)


def _trunk(x, meta_tokens, wts):
    batch, seq, d = x.shape
    m = batch * seq
    x2 = x.reshape(m, d)
    meta = meta_tokens.astype(F32)
    cos, sin = _rope_tables(seq + N_META)

    zm = _norm_matmul(meta, wts["attn_norm"], wts["w_all"], tn=Z_TILE)
    z = _norm_matmul(x2, wts["attn_norm"], wts["w_all"], tn=Z_TILE)

    mla_w = (wts["qa"], wts["kva"], wts["wq"], wts["wkv"], wts["qn"], wts["kn"])
    q, k, v = _mla_prep(z, batch, seq, cos[N_META:], sin[N_META:], *mla_w)
    _, km, vm = _mla_prep(zm, 1, N_META, cos[:N_META], sin[:N_META], *mla_w)
    pad = ((0, 0), (0, META_ROWS - N_META), (0, 0))
    o_a = _attention(q, k, v, jnp.pad(km[0], pad), jnp.pad(vm[0], pad))
    o_a = o_a.reshape(m, MLA_HEADS * V_HEAD)

    lead = jnp.concatenate([jnp.zeros((CHUNK - N_META, Z_RW), F32), zm[:, :Z_RW]], axis=0)
    o_b = _rwkv(z.reshape(batch, seq, -1), lead, wts["shared"], wts["fwd"], wts["bwd"],
                wts["tail"])
    o_b = o_b.reshape(m, RWKV_WIDTH)

    merged = _merge(o_a, o_b, z, wts["w_a"], wts["w_b"])
    h, n2, cmb = _out_router(x2, merged, wts["w_out"], wts["ffn_norm"], wts["wr"], wts["br"])
    out = _moe_dispatch(n2, h, cmb, wts["w_gate"], wts["w_up"], wts["w_down"])
    return out.reshape(batch, seq, d)


def kernel(x_prompt, x_sample, meta_tokens, attn_norm, w_in, q_a_norm, w_q_b, kv_a_norm, w_kv_b, q_norm, k_norm, shift_prev, shift_next, decay_w0_fwd, decay_w2_fwd, decay_w0_bwd, decay_w2_bwd, icl_a0_fwd, icl_a2_fwd, icl_a0_bwd, icl_a2_bwd, gate_w2, k_k, k_a, r_k, ln_x_w, ln_x_b, w_branch_mla, w_branch_rwkv, w_out, ffn_norm, w_router_group, b_router_group, w_router_expert, b_router_expert, w_expert_gate, w_expert_up, w_expert_down):
    p = {
        "attn_norm": attn_norm, "w_in": w_in, "q_a_norm": q_a_norm, "w_q_b": w_q_b,
        "kv_a_norm": kv_a_norm, "w_kv_b": w_kv_b, "q_norm": q_norm, "k_norm": k_norm,
        "shift_prev": shift_prev, "shift_next": shift_next,
        "decay_w0_fwd": decay_w0_fwd, "decay_w2_fwd": decay_w2_fwd,
        "decay_w0_bwd": decay_w0_bwd, "decay_w2_bwd": decay_w2_bwd,
        "icl_a0_fwd": icl_a0_fwd, "icl_a2_fwd": icl_a2_fwd,
        "icl_a0_bwd": icl_a0_bwd, "icl_a2_bwd": icl_a2_bwd,
        "gate_w2": gate_w2, "k_k": k_k, "k_a": k_a, "r_k": r_k, "ln_x_w": ln_x_w, "ln_x_b": ln_x_b,
        "w_branch_mla": w_branch_mla, "w_branch_rwkv": w_branch_rwkv, "w_out": w_out,
        "ffn_norm": ffn_norm, "w_router_group": w_router_group, "b_router_group": b_router_group,
        "w_router_expert": w_router_expert, "b_router_expert": b_router_expert,
        "w_expert_gate": w_expert_gate, "w_expert_up": w_expert_up, "w_expert_down": w_expert_down,
    }
    wts = _prepare_weights({name: val[0] for name, val in p.items()})
    return _trunk(x_prompt, meta_tokens, wts), _trunk(x_sample, meta_tokens, wts)
```

```python
import functools
import math

import jax
import jax.numpy as jnp
from jax import lax
from jax.experimental import pallas as pl
from jax.experimental.pallas import tpu as pltpu
from jax.experimental.pallas import tpu_sc as plsc

F32 = jnp.float32
BF16 = jnp.bfloat16

N_META = 16
RMS_EPS = 1e-6
MLA_HEADS = 8
QK_NOPE = 128
QK_ROPE = 64
QK_HEAD = QK_NOPE + QK_ROPE
V_HEAD = 128
V_WIDE = 256
Q_LORA = 512
KV_LORA = 512
ROPE_THETA = 10000.0
RWKV_HEADS = 16
RWKV_HEAD = 64
RWKV_WIDTH = RWKV_HEADS * RWKV_HEAD
GN_EPS = 64e-5
N_GROUPS = 4
EXPERTS_PER_GROUP = 8
N_EXPERTS = N_GROUPS * EXPERTS_PER_GROUP
D_EXPERT = 512

Z_RW = 3 * RWKV_WIDTH + 4 * 64 + 128
Z_MLA = 1152
Z_GATE = Z_RW + Z_MLA
Z_TILE = 2176

LANES = 128
CHUNK = 64
META_ROWS = 128
ROW_BLOCK = 256
SC_WIN = 128
SC_WORDS = 256
NEG_BIG = -1e30
MIB = 1024 * 1024


def _params(sem, vmem_mib):
    return pltpu.CompilerParams(dimension_semantics=sem, vmem_limit_bytes=vmem_mib * MIB)


def _dot(a, b):
    return jnp.dot(a, b, preferred_element_type=F32)


def _dot_nt(a, b):
    return lax.dot_general(a, b, (((1,), (1,)), ((), ())), preferred_element_type=F32)


def _dot_tn(a, b):
    return lax.dot_general(a, b, (((0,), (0,)), ((), ())), preferred_element_type=F32)


def _split2(x):
    hi = x.astype(BF16)
    lo = (x - hi.astype(F32)).astype(BF16)
    return hi, lo


def _dot_exact_lhs(a_bf16, x):
    hi, lo = _split2(x)
    return _dot(a_bf16, hi) + _dot(a_bf16, lo)


def _dot_wide(x, w):
    x1, x2 = _split2(x)
    w1, w2 = _split2(w)
    return _dot(x1, w1) + (_dot(x1, w2) + _dot(x2, w1))


def _norm_matmul_kernel(x_ref, g_ref, w_ref, o_ref, nb_ref):
    @pl.when(pl.program_id(1) == 0)
    def _():
        x = x_ref[...]
        y = x * lax.rsqrt(jnp.mean(x * x, axis=-1, keepdims=True) + RMS_EPS)
        nb_ref[...] = (y * g_ref[...]).astype(BF16)

    o_ref[...] = _dot(nb_ref[...], w_ref[...])


def _norm_matmul(x, gain, w, tn, tm=512):
    m, d = x.shape
    n = w.shape[1]
    tm = min(tm, m)
    assert m % tm == 0 and n % tn == 0
    return pl.pallas_call(
        _norm_matmul_kernel,
        out_shape=jax.ShapeDtypeStruct((m, n), F32),
        grid=(m // tm, n // tn),
        in_specs=[
            pl.BlockSpec((tm, d), lambda i, j: (i, 0)),
            pl.BlockSpec((1, d), lambda i, j: (0, 0)),
            pl.BlockSpec((d, tn), lambda i, j: (0, j)),
        ],
        out_specs=pl.BlockSpec((tm, tn), lambda i, j: (i, j)),
        scratch_shapes=[pltpu.VMEM((tm, d), BF16)],
        compiler_params=_params(("parallel", "arbitrary"), 48),
        name="norm_matmul",
    )(x, gain.reshape(1, d).astype(F32), w)


def _mla_prep_kernel(z_ref, cos_ref, sin_ref, qa_ref, kva_ref, wq_ref, wkv_ref, qn_ref, kn_ref,
                     q_ref, k_ref, v_ref):
    z = z_ref[...]
    cq = z[:, :Q_LORA]
    ckv = z[:, Q_LORA:Q_LORA + KV_LORA]
    kr = z[:, Q_LORA + KV_LORA:Q_LORA + KV_LORA + QK_ROPE]
    cos = cos_ref[...]
    sin = sin_ref[...]

    def rms(x, g):
        return x * lax.rsqrt(jnp.mean(x * x, axis=-1, keepdims=True) + RMS_EPS) * g

    def rope(x):
        swapped = jnp.concatenate([x[:, QK_ROPE // 2:], x[:, :QK_ROPE // 2]], axis=-1)
        return x * cos + swapped * sin

    q_all = _dot(rms(cq, qa_ref[...]).astype(BF16), wq_ref[...])
    kv_all = _dot(rms(ckv, kva_ref[...]).astype(BF16), wkv_ref[...])
    qn = qn_ref[...]
    kn = kn_ref[...]
    scale = QK_HEAD ** -0.5 * math.log2(math.e)
    ones_col = (lax.broadcasted_iota(jnp.int32, (z.shape[0], V_WIDE - V_HEAD), 1) == 0).astype(BF16)
    kr_sq = jnp.sum(kr * kr, axis=-1, keepdims=True)
    kr_rot = rope(kr * kn[:, QK_NOPE:])
    for h in range(MLA_HEADS):
        qh = q_all[:, h * QK_HEAD:(h + 1) * QK_HEAD]
        qh = qh * (lax.rsqrt(jnp.mean(qh * qh, axis=-1, keepdims=True) + RMS_EPS) * scale) * qn
        q_ref[0, h, :, :QK_NOPE] = qh[:, :QK_NOPE].astype(BF16)
        q_ref[0, h, :, QK_NOPE:] = rope(qh[:, QK_NOPE:]).astype(BF16)
        base = h * (QK_NOPE + V_HEAD)
        k_nope = kv_all[:, base:base + QK_NOPE]
        ms = (jnp.sum(k_nope * k_nope, axis=-1, keepdims=True) + kr_sq) * (1.0 / QK_HEAD)
        inv = lax.rsqrt(ms + RMS_EPS)
        k_ref[0, h, :, :QK_NOPE] = (k_nope * inv * kn[:, :QK_NOPE]).astype(BF16)
        k_ref[0, h, :, QK_NOPE:] = (kr_rot * inv).astype(BF16)
        v_ref[0, h, :, :V_HEAD] = kv_all[:, base + QK_NOPE:base + QK_NOPE + V_HEAD].astype(BF16)
        v_ref[0, h, :, V_HEAD:] = ones_col


def _mla_prep(z, batch, seq, cos, sin, qa, kva, wq, wkv, qn, kn, tm=512):
    tm = min(tm, seq)
    assert seq % tm == 0 and Z_RW % Z_MLA == 0
    nt = seq // tm
    full = lambda b, i: (0, 0)
    return pl.pallas_call(
        _mla_prep_kernel,
        out_shape=(
            jax.ShapeDtypeStruct((batch, MLA_HEADS, seq, QK_HEAD), BF16),
            jax.ShapeDtypeStruct((batch, MLA_HEADS, seq, QK_HEAD), BF16),
            jax.ShapeDtypeStruct((batch, MLA_HEADS, seq, V_WIDE), BF16),
        ),
        grid=(batch, nt),
        in_specs=[
            pl.BlockSpec((tm, Z_MLA), lambda b, i: (b * nt + i, Z_RW // Z_MLA)),
            pl.BlockSpec((tm, QK_ROPE), lambda b, i: (i, 0)),
            pl.BlockSpec((tm, QK_ROPE), lambda b, i: (i, 0)),
            pl.BlockSpec((1, Q_LORA), full),
            pl.BlockSpec((1, KV_LORA), full),
            pl.BlockSpec(wq.shape, full),
            pl.BlockSpec(wkv.shape, full),
            pl.BlockSpec((1, QK_HEAD), full),
            pl.BlockSpec((1, QK_HEAD), full),
        ],
        out_specs=(
            pl.BlockSpec((1, MLA_HEADS, tm, QK_HEAD), lambda b, i: (b, 0, i, 0)),
            pl.BlockSpec((1, MLA_HEADS, tm, QK_HEAD), lambda b, i: (b, 0, i, 0)),
            pl.BlockSpec((1, MLA_HEADS, tm, V_WIDE), lambda b, i: (b, 0, i, 0)),
        ),
        compiler_params=_params(("parallel", "parallel"), 48),
        name="mla_prep",
    )(z, cos, sin, qa, kva, wq, wkv, qn, kn)


def _attention_kernel(q_ref, k_ref, v_ref, km_ref, vm_ref, o_ref, *, tk, n_split):
    q = q_ref[0, 0]
    tq = q.shape[0]
    seq = k_ref.shape[2]

    rows = tq // n_split
    qs = [q[g * rows:(g + 1) * rows] for g in range(n_split)]
    ss = [_dot_nt(x, km_ref[0]) for x in qs]
    lane = lax.broadcasted_iota(jnp.int32, ss[0].shape, 1)
    ss = [jnp.where(lane < N_META, s, NEG_BIG) for s in ss]
    ms = [jnp.max(s, axis=-1, keepdims=True) for s in ss]
    accs = [_dot(jnp.exp2(s - m).astype(BF16), vm_ref[0]) for s, m in zip(ss, ms)]

    def body(c, carry):
        ms, accs = carry
        off = pl.multiple_of(c * tk, tk)
        k = k_ref[0, 0, pl.ds(off, tk), :]
        v = v_ref[0, 0, pl.ds(off, tk), :]
        ss = [_dot_nt(x, k) for x in qs]
        m_new = [jnp.maximum(m, jnp.max(s, axis=-1, keepdims=True)) for m, s in zip(ms, ss)]
        ps = [jnp.exp2(s - m).astype(BF16) for s, m in zip(ss, m_new)]
        accs = [jnp.exp2(m - mn) * a + _dot(p, v) for m, mn, a, p in zip(ms, m_new, accs, ps)]
        return m_new, accs

    ms, accs = lax.fori_loop(0, seq // tk, body, (ms, accs))
    for g in range(n_split):
        acc = accs[g]
        o_ref[0, g * rows:(g + 1) * rows, :] = (
            acc[:, :V_HEAD] / acc[:, V_HEAD:V_HEAD + 1]).astype(BF16)


def _attention(q, k, v, k_meta, v_meta, tq=2048, tk=2048, n_split=4):
    batch, heads, seq, _ = q.shape
    tq = min(tq, seq)
    tk = min(tk, seq)
    assert seq % tq == 0 and seq % tk == 0 and tq % (16 * n_split) == 0
    return pl.pallas_call(
        functools.partial(_attention_kernel, tk=tk, n_split=n_split),
        out_shape=jax.ShapeDtypeStruct((batch, seq, heads * V_HEAD), BF16),
        grid=(batch, heads, seq // tq),
        in_specs=[
            pl.BlockSpec((1, 1, tq, QK_HEAD), lambda b, h, i: (b, h, i, 0)),
            pl.BlockSpec((1, 1, seq, QK_HEAD), lambda b, h, i: (b, h, 0, 0)),
            pl.BlockSpec((1, 1, seq, V_WIDE), lambda b, h, i: (b, h, 0, 0)),
            pl.BlockSpec((1, META_ROWS, QK_HEAD), lambda b, h, i: (h, 0, 0)),
            pl.BlockSpec((1, META_ROWS, V_WIDE), lambda b, h, i: (h, 0, 0)),
        ],
        out_specs=pl.BlockSpec((1, tq, V_HEAD), lambda b, h, i: (b, i, h)),
        compiler_params=_params(("parallel", "parallel", "arbitrary"), 48),
        name="mla_attention",
    )(q, k, v, k_meta, v_meta)


def _head_sum(x, ones_blk):
    xb = x.astype(BF16)
    cols = [_dot(xb[:, c * LANES:(c + 1) * LANES], ones_blk) for c in range(x.shape[1] // LANES)]
    return jnp.concatenate(cols, axis=-1)


def _block_diag(x, bd_mask):
    return jnp.where(bd_mask, jnp.concatenate([x, x], axis=0), jnp.zeros((), x.dtype))


def _tri_inverse(a_list, row, col, bd_mask):
    def same_block(n):
        return (row // n) == (col // n)

    eye = (row == col).astype(F32)
    apow = [jnp.where(same_block(16), a, 0.0) for a in a_list]
    t = [eye - a for a in apow]
    for _ in range(3):
        apow_b = [a.astype(BF16) for a in apow]
        apow = [_dot(a, _block_diag(a, bd_mask)) for a in apow_b]
        t = [ti + _dot(ti.astype(BF16), _block_diag(a.astype(BF16), bd_mask))
             for ti, a in zip(t, apow)]
    n = 32
    while n <= CHUNK:
        off_mask = same_block(n) & jnp.logical_not(same_block(n // 2))
        t_bd = [_block_diag(ti.astype(BF16), bd_mask) for ti in t]
        inner = [_dot(jnp.where(off_mask, a, 0.0).astype(BF16), tb).astype(BF16)
                 for a, tb in zip(a_list, t_bd)]
        t = [ti - _dot(ti.astype(BF16), _block_diag(x, bd_mask)) for ti, x in zip(t, inner)]
        n *= 2
    return t


def _rwkv_kernel(*refs, reverse, n_lead, nb):
    if reverse:
        (zs_ref, kkn_ref, yf_ref, bf_ref, w0_ref, w2_ref, a0_ref, a2_ref, ka_ref, rk_ref,
         g2_ref, lnw_ref, lnb_ref, o_ref, st_ref) = refs
    else:
        (z_ref, zp_ref, zn_ref, lead_ref, sp_ref, sn_ref, w0_ref, w2_ref, a0_ref, a2_ref,
         kk_ref, ka_ref, rk_ref, y_ref, bon_ref, zs_ref, kkn_ref, st_ref) = refs
    c = pl.program_id(1)
    nc = pl.num_programs(1)
    phys = (nc - c) if reverse else c

    @pl.when(c == 0)
    def _():
        st_ref[...] = jnp.zeros_like(st_ref)

    w = RWKV_WIDTH
    rows = lax.broadcasted_iota(jnp.int32, (CHUNK, 1), 0)
    lane_r = lax.broadcasted_iota(jnp.int32, (LANES, LANES), 0)
    lane_c = lax.broadcasted_iota(jnp.int32, (LANES, LANES), 1)
    bd_mask = (lane_r // RWKV_HEAD) == (lane_c // RWKV_HEAD)
    ones_blk = bd_mask.astype(BF16)
    row = lax.broadcasted_iota(jnp.int32, (CHUNK, LANES), 0)
    lane = lax.broadcasted_iota(jnp.int32, (CHUNK, LANES), 1)
    col = lane % CHUNK
    first_head = lane < RWKV_HEAD
    if reverse:
        incl = col >= row
        strict = col > row
    else:
        incl = col <= row
        strict = col < row
    tri = incl[:, :CHUNK].astype(BF16)

    def operands(i):
        if reverse:
            zs = zs_ref[i]
            kkn = kkn_ref[i]
            r, k, v = zs[:, :w], zs[:, w:2 * w], zs[:, 2 * w:3 * w]
        else:
            z = jnp.where(phys == 0, lead_ref[...], z_ref[i])
            prev_row = jnp.where(phys == 1, lead_ref[CHUNK - 1:CHUNK, :], zp_ref[i, 7:8, :])
            prev_row = jnp.where(phys > 0, prev_row, 0.0)
            next_row = jnp.where(phys < nc - 1, zn_ref[i, 0:1, :], 0.0)
            z_prev = jnp.where(rows == 0, prev_row, pltpu.roll(z, 1, axis=0))
            z_next = jnp.where(rows == CHUNK - 1, next_row, pltpu.roll(z, CHUNK - 1, axis=0))
            zs = z + sp_ref[...] * (z_prev - z) + sn_ref[...] * (z_next - z)
            zs_ref[i] = zs
            live = jnp.logical_or(phys > 0, rows >= n_lead).astype(F32)
            r = zs[:, :w] * live
            k = zs[:, w:2 * w] * live
            v = zs[:, 2 * w:3 * w] * live
            kk = k * kk_ref[...]
            kkn = kk * jnp.minimum(lax.rsqrt(_head_sum(kk * kk, ones_blk)), 1e12)
            kkn_ref[i] = kkn
        wd = zs[:, 3 * w:3 * w + LANES]
        ad = zs[:, 3 * w + LANES:3 * w + 2 * LANES]
        w_raw = w0_ref[...] + _dot(jnp.tanh(wd).astype(BF16), w2_ref[...])
        lw = (-math.exp(-0.5)) * jax.nn.sigmoid(w_raw)
        icl = jax.nn.sigmoid(a0_ref[...] + _dot(ad.astype(BF16), a2_ref[...]))
        k_rep = k * (1.0 + (icl - 1.0) * ka_ref[...])
        bvec = kkn * icl
        bonus = _head_sum(r * k_rep * rk_ref[...], ones_blk) * v
        l_in = _dot_exact_lhs(tri, lw)
        l_tot = l_in[0:1, :] if reverse else l_in[CHUNK - 1:CHUNK, :]
        e_neg = jnp.exp(-l_in)
        e_rem = jnp.exp(l_tot - l_in)
        ops = {
            "kd": (kkn * jnp.exp(l_in - lw)).astype(BF16), "rd": (r * jnp.exp(l_in)).astype(BF16),
            "bn": (bvec * e_neg).astype(BF16), "kn": (k_rep * e_neg).astype(BF16),
            "bh": (bvec * e_rem).astype(BF16), "kh": (k_rep * e_rem).astype(BF16),
            "v": v.astype(BF16),
        }
        p_tot_t = jnp.transpose(jnp.broadcast_to(jnp.exp(l_tot), (8, w)))
        return ops, p_tot_t, bonus, zs

    per_seq = [operands(i) for i in range(nb)]

    n_pairs = RWKV_HEADS // 2
    items = [(i, p) for i in range(nb) for p in range(n_pairs)]
    bd = lambda x: _block_diag(x, bd_mask)
    diag_blocks = lambda x: jnp.where(first_head, x[:RWKV_HEAD], x[RWKV_HEAD:])
    kd_p, rd_p, bn_p, kn_p, bh_p, kh_p, v_p = (
        [per_seq[i][0][name][:, p * LANES:(p + 1) * LANES] for i, p in items]
        for name in ("kd", "rd", "bn", "kn", "bh", "kh", "v"))
    v_bd = [bd(x) for x in v_p]
    kr_p = [jnp.concatenate([a, b], axis=0) for a, b in zip(kd_p, rd_p)]
    prod_b = [_dot_nt(x, bd(y)) for x, y in zip(kr_p, bn_p)]
    prod_k = [_dot_nt(x, bd(y)) for x, y in zip(kr_p, kn_p)]
    a_ab = [jnp.where(strict, x[:CHUNK], 0.0) for x in prod_b]
    m_rb = [jnp.where(incl, x[CHUNK:], 0.0).astype(BF16) for x in prod_b]
    a_ak = [jnp.where(strict, x[:CHUNK], 0.0).astype(BF16) for x in prod_k]
    m_rk = [jnp.where(incl, x[CHUNK:], 0.0).astype(BF16) for x in prod_k]
    akv = [_dot(a, vv).astype(BF16) for a, vv in zip(a_ak, v_bd)]
    t_inv = [t.astype(BF16) for t in _tri_inverse(a_ab, row, col, bd_mask)]
    wmat = [(-_dot(t, bd(x))).astype(BF16) for t, x in zip(t_inv, kd_p)]
    u0 = [(-_dot(t, bd(x))).astype(BF16) for t, x in zip(t_inv, akv)]
    u0_bd = [bd(x) for x in u0]
    st = [st_ref[i * n_pairs + p] for i, p in items]
    st_bd = [bd(s.astype(BF16)) for s in st]
    q_mat = [(x.astype(F32) + _dot(m, bd(wm))).astype(BF16)
             for x, m, wm in zip(rd_p, m_rb, wmat)]
    ys = [_dot(qm, sb) + _dot(mb, u) + _dot(mk, vv)
          for qm, sb, mb, u, mk, vv in zip(q_mat, st_bd, m_rb, u0_bd, m_rk, v_bd)]
    g_mat = [diag_blocks(_dot_tn(b, wm)).astype(BF16) for b, wm in zip(bh_p, wmat)]
    h_mat = [diag_blocks(_dot_tn(b, u) + _dot_tn(kx, vv))
             for b, u, kx, vv in zip(bh_p, u0, kh_p, v_p)]
    for n, (i, p) in enumerate(items):
        p_tot_t = per_seq[i][1]
        decay = jnp.where(first_head, p_tot_t[2 * p * RWKV_HEAD:(2 * p + 1) * RWKV_HEAD, 0:1],
                          p_tot_t[(2 * p + 1) * RWKV_HEAD:(2 * p + 2) * RWKV_HEAD, 0:1])
        st_ref[i * n_pairs + p] = decay * st[n] + _dot(g_mat[n], st_bd[n]) + h_mat[n]

    for i in range(nb):
        y = jnp.concatenate(ys[i * n_pairs:(i + 1) * n_pairs], axis=-1)
        _, _, bonus, zs = per_seq[i]
        if not reverse:
            y_ref[i] = y
            bon_ref[i] = bonus
        else:
            y = y + yf_ref[i]
            mu = _head_sum(y, ones_blk) * (1.0 / RWKV_HEAD)
            yc = y - mu
            var = _head_sum(yc * yc, ones_blk) * (1.0 / RWKV_HEAD)
            yn = (yc * lax.rsqrt(var + GN_EPS) * lnw_ref[...] + lnb_ref[...]
                  + (bonus + bf_ref[i]))
            gd = zs[:, 3 * w + 2 * LANES:3 * w + 3 * LANES]
            g = _dot(jax.nn.sigmoid(gd).astype(BF16), g2_ref[...])
            o_ref[i] = (yn * g).astype(BF16)


def _rwkv(z_rw, lead, shared, fwd_p, bwd_p, tail, nb=2):
    batch, seq, _ = z_rw.shape
    zc = Z_RW
    nb = min(nb, batch)
    assert batch % nb == 0 and seq % CHUNK == 0
    ncx = seq // CHUNK
    lp = seq + CHUNK
    sub = CHUNK // 8
    w = RWKV_WIDTH
    n_lead = CHUNK - N_META
    full = lambda b, c: (0, 0)
    row_w = pl.BlockSpec((1, w), full)
    mat_w = pl.BlockSpec((LANES, w), full)
    state = pltpu.VMEM((nb * RWKV_HEADS // 2, RWKV_HEAD, 2 * RWKV_HEAD), F32)

    chunk = lambda b, c: (b, c, 0)
    fwd_specs = [
        pl.BlockSpec((nb, CHUNK, zc), lambda b, c: (b, jnp.maximum(c - 1, 0), 0)),
        pl.BlockSpec((nb, 8, zc), lambda b, c: (b, jnp.maximum((c - 1) * sub - 1, 0), 0)),
        pl.BlockSpec((nb, 8, zc), lambda b, c: (b, jnp.minimum(c * sub, ncx * sub - 1), 0)),
        pl.BlockSpec((CHUNK, zc), full),
        pl.BlockSpec((1, zc), full), pl.BlockSpec((1, zc), full),
        row_w, mat_w, row_w, mat_w, row_w, row_w, row_w,
    ]
    fwd_args = [z_rw, z_rw, z_rw, lead, shared["sp"], shared["sn"], fwd_p["w0"], fwd_p["w2"],
                fwd_p["a0"], fwd_p["a2"], shared["k_k"], shared["k_a"], shared["r_k"]]
    y_f, bon_f, zs, kkn = pl.pallas_call(
        functools.partial(_rwkv_kernel, reverse=False, n_lead=n_lead, nb=nb),
        out_shape=(jax.ShapeDtypeStruct((batch, lp, w), F32),
                   jax.ShapeDtypeStruct((batch, lp, w), F32),
                   jax.ShapeDtypeStruct((batch, lp, zc), F32),
                   jax.ShapeDtypeStruct((batch, lp, w), F32)),
        grid=(batch // nb, ncx + 1),
        in_specs=fwd_specs,
        out_specs=(pl.BlockSpec((nb, CHUNK, w), chunk), pl.BlockSpec((nb, CHUNK, w), chunk),
                   pl.BlockSpec((nb, CHUNK, zc), chunk), pl.BlockSpec((nb, CHUNK, w), chunk)),
        scratch_shapes=[state],
        compiler_params=_params(("parallel", "arbitrary"), 48),
        name="rwkv7_fwd",
    )(*fwd_args)

    rchunk = lambda b, c: (b, ncx - c, 0)
    bwd_specs = [
        pl.BlockSpec((nb, CHUNK, zc), rchunk), pl.BlockSpec((nb, CHUNK, w), rchunk),
        pl.BlockSpec((nb, CHUNK, w), rchunk), pl.BlockSpec((nb, CHUNK, w), rchunk),
        row_w, mat_w, row_w, mat_w, row_w, row_w, mat_w, row_w, row_w,
    ]
    bwd_args = [zs, kkn, y_f, bon_f, bwd_p["w0"], bwd_p["w2"], bwd_p["a0"], bwd_p["a2"],
                shared["k_a"], shared["r_k"], tail["g2"], tail["ln_w"], tail["ln_b"]]
    return pl.pallas_call(
        functools.partial(_rwkv_kernel, reverse=True, n_lead=n_lead, nb=nb),
        out_shape=jax.ShapeDtypeStruct((batch, seq, w), BF16),
        grid=(batch // nb, ncx),
        in_specs=bwd_specs,
        out_specs=pl.BlockSpec((nb, CHUNK, w), lambda b, c: (b, ncx - 1 - c, 0)),
        scratch_shapes=[state],
        compiler_params=_params(("parallel", "arbitrary"), 48),
        name="rwkv7_bwd",
    )(*bwd_args)


def _pack_rows(x):
    half = x.shape[1] // 2
    bits = lax.bitcast_convert_type(x.astype(BF16).astype(F32), jnp.uint32)
    words = bits[:, :half] | (bits[:, half:] >> 16)
    return [words[:, s * SC_WORDS:(s + 1) * SC_WORDS] for s in range(half // SC_WORDS)]


def _unpack_rows(slabs):
    words = jnp.concatenate(slabs, axis=1)
    hi = lax.bitcast_convert_type(words & jnp.uint32(0xFFFF0000), F32)
    lo = lax.bitcast_convert_type(words << 16, F32)
    return jnp.concatenate([hi, lo], axis=1)


def _out_router_kernel(*refs, n_gate):
    oa_ref, ob_ref = refs[:2]
    ga_refs = refs[2:2 + n_gate]
    gb_refs = refs[2 + n_gate:2 + 2 * n_gate]
    (x_ref, wa_ref, wb_ref, wo_ref, g_ref, wr_ref, br_ref,
     h_ref, n2_ref, cmb_ref) = refs[2 + 2 * n_gate:]
    gate_a = jnp.concatenate([jax.nn.sigmoid(r[...]) for r in ga_refs], axis=1)
    gate_b = jnp.concatenate([jax.nn.sigmoid(r[...]) for r in gb_refs], axis=1)
    merged = (gate_a * _dot(oa_ref[...], wa_ref[...])
              + gate_b * _dot(ob_ref[...], wb_ref[...])).astype(BF16)
    h = x_ref[...] + _dot(merged, wo_ref[...])
    h_ref[...] = h
    n2 = h * lax.rsqrt(jnp.mean(h * h, axis=-1, keepdims=True) + RMS_EPS) * g_ref[...]
    for s, slab in enumerate(_pack_rows(n2)):
        n2_ref[s] = slab
    logits = _dot_wide(n2, wr_ref[...]) + br_ref[...]
    lane = lax.broadcasted_iota(jnp.int32, logits.shape, 1)
    big = jnp.int32(1 << 20)
    g_mask = (lane >= N_EXPERTS) & (lane < N_EXPERTS + N_GROUPS)
    gl = jnp.where(g_mask, logits, NEG_BIG)
    g_max = jnp.max(gl, axis=-1, keepdims=True)
    pg_top = 1.0 / jnp.sum(jnp.where(g_mask, jnp.exp(gl - g_max), 0.0), axis=-1, keepdims=True)
    gi = jnp.min(jnp.where(gl == g_max, lane, big), axis=-1, keepdims=True) - N_EXPERTS
    e_mask = (lane >= gi * EXPERTS_PER_GROUP) & (lane < (gi + 1) * EXPERTS_PER_GROUP)
    el = jnp.where(e_mask, logits, NEG_BIG)
    m1 = jnp.max(el, axis=-1, keepdims=True)
    i1 = jnp.min(jnp.where(el == m1, lane, big), axis=-1, keepdims=True)
    el2 = jnp.where(lane == i1, NEG_BIG, el)
    m2 = jnp.max(el2, axis=-1, keepdims=True)
    i2 = jnp.min(jnp.where(el2 == m2, lane, big), axis=-1, keepdims=True)
    e21 = jnp.exp(m2 - m1)
    gate1 = pg_top / (1.0 + e21)
    gate2 = pg_top * e21 / (1.0 + e21)
    cmb_ref[...] = jnp.where(lane == i1, gate1, 0.0) + jnp.where(lane == i2, gate2, 0.0)


def _out_router(x, o_a, o_b, z, w_a, w_b, w_out, gain, w_r, b_r, tm=256, tg=512):
    m, d = x.shape
    tm = min(tm, m)
    assert Z_GATE % tg == 0 and d % tg == 0
    n_gate = d // tg
    g0 = Z_GATE // tg
    slabs = d // 2 // SC_WORDS
    row = lambda i: (i, 0)
    once = pl.Buffered(1)
    const = lambda shape: pl.BlockSpec(shape, lambda i: (0, 0), pipeline_mode=once)
    gate_specs = [pl.BlockSpec((tm, tg), functools.partial(lambda i, c: (i, c), c=g0 + j))
                  for j in range(2 * n_gate)]
    return pl.pallas_call(
        functools.partial(_out_router_kernel, n_gate=n_gate),
        out_shape=(jax.ShapeDtypeStruct((m, d), F32),
                   jax.ShapeDtypeStruct((slabs, m, SC_WORDS), jnp.uint32),
                   jax.ShapeDtypeStruct((m, LANES), F32)),
        grid=(m // tm,),
        in_specs=[pl.BlockSpec((tm, o_a.shape[1]), row), pl.BlockSpec((tm, o_b.shape[1]), row)]
        + gate_specs
        + [pl.BlockSpec((tm, d), row), const(w_a.shape), const(w_b.shape), const((d, d)),
           const((1, d)), const((d, LANES)), const((1, LANES))],
        out_specs=(pl.BlockSpec((tm, d), row),
                   pl.BlockSpec((slabs, tm, SC_WORDS), lambda i: (0, i, 0)),
                   pl.BlockSpec((tm, LANES), row)),
        compiler_params=_params(("parallel",), 48),
        name="out_proj_router",
    )(o_a, o_b, *([z] * (2 * n_gate)), x, w_a, w_b, w_out, gain, w_r, b_r)


def _expert_count_kernel(cmb_ref, cnt_ref):
    @pl.when(pl.program_id(0) == 0)
    def _():
        cnt_ref[...] = jnp.zeros_like(cnt_ref)

    sel = (cmb_ref[...] > 0.0).astype(F32)
    cnt_ref[...] += jnp.sum(sel.reshape(-1, 8, LANES), axis=0)


def _expert_counts(cmb, tm=2048):
    m = cmb.shape[0]
    tm = min(tm, m)
    return pl.pallas_call(
        _expert_count_kernel,
        out_shape=jax.ShapeDtypeStruct((8, LANES), F32),
        grid=(m // tm,),
        in_specs=[pl.BlockSpec((tm, LANES), lambda i: (i, 0))],
        out_specs=pl.BlockSpec((8, LANES), lambda i: (0, 0)),
        compiler_params=_params(("arbitrary",), 32),
        name="moe_expert_counts",
    )(cmb)


def _dispatch_index_kernel(cmb_ref, tri_ref, tril_ref, off_ref, dest_ref, gw_ref, base_ref, *, spare):
    @pl.when(pl.program_id(0) == 0)
    def _():
        base_ref[...] = jnp.zeros_like(base_ref)

    tm = cmb_ref.shape[0]
    seg = tri_ref.shape[0]
    base = base_ref[:, 0:1]
    which = lax.broadcasted_iota(jnp.int32, (LANES, seg), 0)
    for s in range(tm // seg):
        cmb_t = jnp.transpose(cmb_ref[s * seg:(s + 1) * seg, :])
        sel = cmb_t > 0.0
        sel_b = sel.astype(BF16)
        pos = _dot(sel_b, tri_ref[...]) + base
        base = pos[:, seg - 1:seg]
        nth = _dot(tril_ref[...], sel_b)
        row = off_ref[...] + pos - 1.0
        gws = []
        for k in range(2):
            pick = sel & (nth == float(k + 1))
            found = jnp.sum(jnp.where(pick, 1.0, 0.0), axis=0, keepdims=True) > 0.0
            dst = jnp.sum(jnp.where(pick, row, 0.0), axis=0, keepdims=True)
            dest_ref[k:k + 1, s * seg:(s + 1) * seg] = jnp.where(found, dst, spare).astype(jnp.int32)
            gws.append(jnp.sum(jnp.where(pick, cmb_t, 0.0), axis=0, keepdims=True))
        gmat = jnp.where(which == 0, gws[0], jnp.where(which == 1, gws[1], 0.0))
        gw_ref[s * seg:(s + 1) * seg, :] = jnp.transpose(gmat)
    base_ref[...] = jnp.broadcast_to(base, base_ref.shape)


def _dispatch_index(cmb, off_col, spare, tm=2048, seg=1024):
    m = cmb.shape[0]
    tm = min(tm, m)
    seg = min(seg, tm)
    idx = jnp.arange(seg)
    tri = (idx[:, None] <= idx[None, :]).astype(BF16)
    lid = jnp.arange(LANES)
    tril = (lid[:, None] >= lid[None, :]).astype(BF16)
    return pl.pallas_call(
        functools.partial(_dispatch_index_kernel, spare=float(spare)),
        out_shape=(jax.ShapeDtypeStruct((2, m), jnp.int32), jax.ShapeDtypeStruct((m, LANES), F32)),
        grid=(m // tm,),
        in_specs=[pl.BlockSpec((tm, LANES), lambda i: (i, 0)),
                  pl.BlockSpec((seg, seg), lambda i: (0, 0)),
                  pl.BlockSpec((LANES, LANES), lambda i: (0, 0)),
                  pl.BlockSpec((LANES, 1), lambda i: (0, 0))],
        out_specs=(pl.BlockSpec((2, tm), lambda i: (0, i)), pl.BlockSpec((tm, LANES), lambda i: (i, 0))),
        scratch_shapes=[pltpu.VMEM((LANES, LANES), F32)],
        compiler_params=_params(("arbitrary",), 32),
        name="moe_dispatch_index",
    )(cmb, tri, tril, off_col)


def _sc_scatter_rows(src, idx, n_dst):
    n_src, words = src.shape
    n_idx = idx.shape[0]
    n_win = n_src // SC_WIN
    mesh = plsc.VectorSubcoreMesh(core_axis_name="c", subcore_axis_name="s")

    @pl.kernel(out_type=jax.ShapeDtypeStruct((n_dst, words), src.dtype), mesh=mesh,
               scratch_types=[])
    def scatter(x_hbm, i_hbm, o_hbm):
        def body(x_vmem, i_vmem):
            pltpu.sync_copy(x_vmem, o_hbm.at[i_vmem.at[0]])

        pltpu.emit_pipeline(
            body, grid=(n_idx // SC_WIN,),
            in_specs=[pl.BlockSpec((SC_WIN, words), index_map=lambda i: (i % n_win, 0)),
                      pl.BlockSpec((1, SC_WIN), index_map=lambda i: (0, i))],
            out_specs=[],
            core_axis_name=("c", "s"), dimension_semantics=(pltpu.PARALLEL,),
        )(x_hbm, i_hbm)

    return scatter(src, idx.reshape(1, n_idx))


def _sc_gather_rows(src, idx):
    words = src.shape[1]
    n_idx = idx.shape[0]
    mesh = plsc.VectorSubcoreMesh(core_axis_name="c", subcore_axis_name="s")

    @pl.kernel(out_type=jax.ShapeDtypeStruct((n_idx, words), src.dtype), mesh=mesh)
    def gather(x_hbm, i_hbm, o_hbm):
        def body(i_vmem, o_vmem):
            pltpu.sync_copy(x_hbm.at[i_vmem.at[0]], o_vmem)

        pltpu.emit_pipeline(
            body, grid=(n_idx // SC_WIN,),
            in_specs=[pl.BlockSpec((1, SC_WIN), index_map=lambda i: (0, i))],
            out_specs=[pl.BlockSpec((SC_WIN, words), index_map=lambda i: (i, 0))],
            core_axis_name=("c", "s"), dimension_semantics=(pltpu.PARALLEL,),
        )(i_hbm, o_hbm)

    return gather(src, idx.reshape(1, n_idx))


def _expert_block_kernel(blk_e_ref, n_used_ref, xs_ref, wg_ref, wu_ref, wd_ref, ys_ref):
    @pl.when(pl.program_id(0) < n_used_ref[0])
    def _():
        xe = _unpack_rows([xs_ref[s] for s in range(xs_ref.shape[0])]).astype(BF16)
        hg = _dot(xe, wg_ref[0])
        hu = _dot(xe, wu_ref[0])
        act = (hg * jax.nn.sigmoid(hg)) * hu
        for s, slab in enumerate(_pack_rows(_dot(act.astype(BF16), wd_ref[0]))):
            ys_ref[s] = slab


def _expert_blocks(blk_e, n_used, xs, w_gate, w_up, w_down):
    slabs, rows, words = xs.shape
    _, d, f = w_gate.shape
    nb = rows // ROW_BLOCK - 1
    packed = pl.BlockSpec((slabs, ROW_BLOCK, words), lambda j, be, nu: (0, j, 0))
    grid_spec = pltpu.PrefetchScalarGridSpec(
        num_scalar_prefetch=2,
        grid=(nb,),
        in_specs=[packed,
                  pl.BlockSpec((1, d, f), lambda j, be, nu: (be[j], 0, 0)),
                  pl.BlockSpec((1, d, f), lambda j, be, nu: (be[j], 0, 0)),
                  pl.BlockSpec((1, f, d), lambda j, be, nu: (be[j], 0, 0))],
        out_specs=packed,
    )
    return pl.pallas_call(
        _expert_block_kernel,
        out_shape=jax.ShapeDtypeStruct(xs.shape, xs.dtype),
        grid_spec=grid_spec,
        compiler_params=_params(("arbitrary",), 48),
        name="moe_expert_blocks",
    )(blk_e, n_used, xs, w_gate, w_up, w_down)


def _combine_kernel(h_ref, y0_ref, y1_ref, gw_ref, o_ref):
    gw = gw_ref[...]
    w0 = gw[:, 0:1]
    w1 = gw[:, 1:2]
    n_slabs = y0_ref.shape[1]
    y0 = jnp.where(w0 > 0.0, w0 * _unpack_rows([y0_ref[0, s] for s in range(n_slabs)]), 0.0)
    y1 = jnp.where(w1 > 0.0, w1 * _unpack_rows([y1_ref[0, s] for s in range(n_slabs)]), 0.0)
    o_ref[...] = h_ref[...] + (y0 + y1)


def _combine(h, y, gw, tm=512):
    m, d = h.shape
    tm = min(tm, m)
    _, slabs, _, words = y.shape
    row = lambda i: (i, 0)
    return pl.pallas_call(
        _combine_kernel,
        out_shape=jax.ShapeDtypeStruct((m, d), F32),
        grid=(m // tm,),
        in_specs=[pl.BlockSpec((tm, d), row),
                  pl.BlockSpec((1, slabs, tm, words), lambda i: (0, 0, i, 0)),
                  pl.BlockSpec((1, slabs, tm, words), lambda i: (1, 0, i, 0)),
                  pl.BlockSpec((tm, LANES), row)],
        out_specs=pl.BlockSpec((tm, d), row),
        compiler_params=_params(("parallel",), 48),
        name="moe_combine",
    )(h, y, y, gw)


def _moe_dispatch(n2p, h, cmb, w_gate, w_up, w_down):
    slabs, m, words = n2p.shape
    cnt = jnp.sum(_expert_counts(cmb), axis=0)[:N_EXPERTS].astype(jnp.int32)
    nblk = (cnt + ROW_BLOCK - 1) // ROW_BLOCK
    cum = jnp.cumsum(nblk)
    nb = 2 * m // ROW_BLOCK + N_EXPERTS
    off = ((cum - nblk) * ROW_BLOCK).astype(F32)
    off_col = jnp.zeros((LANES, 1), F32).at[:N_EXPERTS, 0].set(off)
    blk_e = jnp.sum(cum[None, :] <= jnp.arange(nb, dtype=jnp.int32)[:, None], axis=1)
    blk_e = jnp.minimum(blk_e, N_EXPERTS - 1).astype(jnp.int32)
    n_used = cum[-1:].astype(jnp.int32)
    spare = nb * ROW_BLOCK
    rows = spare + ROW_BLOCK
    dest, gw = _dispatch_index(cmb, off_col, spare)
    idx = (dest[:, None, :] + (jnp.arange(slabs, dtype=jnp.int32) * rows)[None, :, None]).reshape(-1)
    xs = _sc_scatter_rows(n2p.reshape(slabs * m, words), idx, slabs * rows)
    ys = _expert_blocks(blk_e, n_used, xs.reshape(slabs, rows, words), w_gate, w_up, w_down)
    y = _sc_gather_rows(ys.reshape(slabs * rows, words), idx)
    return _combine(h, y.reshape(2, slabs, m, words), gw)


def _prepare_weights(p):
    d = p["w_in"].shape[0]
    splits = [Q_LORA, KV_LORA, QK_ROPE, 3 * RWKV_WIDTH + 4 * 64 + 128, d, d]
    offs = [0]
    for s in splits:
        offs.append(offs[-1] + s)
    w_in = p["w_in"].astype(BF16)
    zeros64 = jnp.zeros((64, RWKV_WIDTH), F32)

    def pad_lo(a):
        return jnp.concatenate([a, zeros64], axis=0).astype(BF16)

    def pad_hi(a):
        return jnp.concatenate([zeros64, a], axis=0).astype(BF16)

    row = lambda a: a.reshape(1, -1).astype(F32)
    wr = jnp.concatenate(
        [jnp.transpose(p["w_router_expert"], (1, 0, 2)).reshape(d, N_EXPERTS), p["w_router_group"],
         jnp.zeros((d, LANES - N_EXPERTS - N_GROUPS), F32)], axis=1)
    br = jnp.concatenate([p["b_router_expert"].reshape(-1), p["b_router_group"],
                          jnp.zeros((LANES - N_EXPERTS - N_GROUPS,), F32)]).reshape(1, LANES)
    return {
        "attn_norm": p["attn_norm"],
        "w_all": jnp.concatenate(
            [w_in[:, offs[3]:offs[4]], w_in[:, offs[0]:offs[3]],
             jnp.zeros((d, Z_MLA - (offs[3] - offs[0])), BF16), w_in[:, offs[4]:offs[6]]], axis=1),
        "qa": row(p["q_a_norm"]), "kva": row(p["kv_a_norm"]),
        "wq": p["w_q_b"].astype(BF16), "wkv": p["w_kv_b"].astype(BF16),
        "qn": row(p["q_norm"]), "kn": row(p["k_norm"]),
        "shared": {"sp": row(p["shift_prev"]), "sn": row(p["shift_next"]), "k_k": row(p["k_k"]),
                   "k_a": row(p["k_a"]), "r_k": row(p["r_k"])},
        "fwd": {"w0": row(p["decay_w0_fwd"]), "w2": pad_lo(p["decay_w2_fwd"]),
                "a0": row(p["icl_a0_fwd"]), "a2": pad_lo(p["icl_a2_fwd"])},
        "bwd": {"w0": row(p["decay_w0_bwd"]), "w2": pad_hi(p["decay_w2_bwd"]),
                "a0": row(p["icl_a0_bwd"]), "a2": pad_hi(p["icl_a2_bwd"])},
        "tail": {"g2": p["gate_w2"].astype(BF16), "ln_w": row(p["ln_x_w"]), "ln_b": row(p["ln_x_b"])},
        "w_a": p["w_branch_mla"].astype(BF16), "w_b": p["w_branch_rwkv"].astype(BF16),
        "w_out": p["w_out"].astype(BF16), "ffn_norm": row(p["ffn_norm"]),
        "wr": wr, "br": br,
        "w_gate": p["w_expert_gate"].astype(BF16), "w_up": p["w_expert_up"].astype(BF16),
        "w_down": p["w_expert_down"].astype(BF16),
    }


def _rope_tables(length):
    pos = jnp.arange(length, dtype=F32)
    inv = ROPE_THETA ** (-jnp.arange(0, QK_ROPE, 2, dtype=F32) / QK_ROPE)
    ang = pos[:, None] * inv[None, :]
    cos, sin = jnp.cos(ang), jnp.sin(ang)
    return jnp.concatenate([cos, cos], axis=-1), jnp.concatenate([-sin, sin], axis=-1)


def _trunk(x, meta_tokens, wts):
    batch, seq, d = x.shape
    m = batch * seq
    x2 = x.reshape(m, d)
    meta = meta_tokens.astype(F32)
    cos, sin = _rope_tables(seq + N_META)

    zm = _norm_matmul(meta, wts["attn_norm"], wts["w_all"], tn=Z_TILE)
    z = _norm_matmul(x2, wts["attn_norm"], wts["w_all"], tn=Z_TILE)

    mla_w = (wts["qa"], wts["kva"], wts["wq"], wts["wkv"], wts["qn"], wts["kn"])
    q, k, v = _mla_prep(z, batch, seq, cos[N_META:], sin[N_META:], *mla_w)
    _, km, vm = _mla_prep(zm, 1, N_META, cos[:N_META], sin[:N_META], *mla_w)
    pad = ((0, 0), (0, META_ROWS - N_META), (0, 0))
    o_a = _attention(q, k, v, jnp.pad(km[0], pad), jnp.pad(vm[0], pad))
    o_a = o_a.reshape(m, MLA_HEADS * V_HEAD)

    lead = jnp.concatenate([jnp.zeros((CHUNK - N_META, Z_RW), F32), zm[:, :Z_RW]], axis=0)
    o_b = _rwkv(z.reshape(batch, seq, -1), lead, wts["shared"], wts["fwd"], wts["bwd"],
                wts["tail"])
    o_b = o_b.reshape(m, RWKV_WIDTH)

    h, n2, cmb = _out_router(x2, o_a, o_b, z, wts["w_a"], wts["w_b"], wts["w_out"],
                             wts["ffn_norm"], wts["wr"], wts["br"])
    out = _moe_dispatch(n2, h, cmb, wts["w_gate"], wts["w_up"], wts["w_down"])
    return out.reshape(batch, seq, d)


def kernel(x_prompt, x_sample, meta_tokens, attn_norm, w_in, q_a_norm, w_q_b, kv_a_norm, w_kv_b, q_norm, k_norm, shift_prev, shift_next, decay_w0_fwd, decay_w2_fwd, decay_w0_bwd, decay_w2_bwd, icl_a0_fwd, icl_a2_fwd, icl_a0_bwd, icl_a2_bwd, gate_w2, k_k, k_a, r_k, ln_x_w, ln_x_b, w_branch_mla, w_branch_rwkv, w_out, ffn_norm, w_router_group, b_router_group, w_router_expert, b_router_expert, w_expert_gate, w_expert_up, w_expert_down):
    p = {
        "attn_norm": attn_norm, "w_in": w_in, "q_a_norm": q_a_norm, "w_q_b": w_q_b,
        "kv_a_norm": kv_a_norm, "w_kv_b": w_kv_b, "q_norm": q_norm, "k_norm": k_norm,
        "shift_prev": shift_prev, "shift_next": shift_next,
        "decay_w0_fwd": decay_w0_fwd, "decay_w2_fwd": decay_w2_fwd,
        "decay_w0_bwd": decay_w0_bwd, "decay_w2_bwd": decay_w2_bwd,
        "icl_a0_fwd": icl_a0_fwd, "icl_a2_fwd": icl_a2_fwd,
        "icl_a0_bwd": icl_a0_bwd, "icl_a2_bwd": icl_a2_bwd,
        "gate_w2": gate_w2, "k_k": k_k, "k_a": k_a, "r_k": r_k, "ln_x_w": ln_x_w, "ln_x_b": ln_x_b,
        "w_branch_mla": w_branch_mla, "w_branch_rwkv": w_branch_rwkv, "w_out": w_out,
        "ffn_norm": ffn_norm, "w_router_group": w_router_group, "b_router_group": b_router_group,
        "w_router_expert": w_router_expert, "b_router_expert": b_router_expert,
        "w_expert_gate": w_expert_gate, "w_expert_up": w_expert_up, "w_expert_down": w_expert_down,
    }
    wts = _prepare_weights({name: val[0] for name, val in p.items()})
    return _trunk(x_prompt, meta_tokens, wts), _trunk(x_sample, meta_tokens, wts)
```

```python
import functools
import math

import jax
import jax.numpy as jnp
from jax import lax
from jax.experimental import pallas as pl
from jax.experimental.pallas import tpu as pltpu
from jax.experimental.pallas import tpu_sc as plsc

F32 = jnp.float32
BF16 = jnp.bfloat16

N_META = 16
RMS_EPS = 1e-6
MLA_HEADS = 8
QK_NOPE = 128
QK_ROPE = 64
QK_HEAD = QK_NOPE + QK_ROPE
V_HEAD = 128
V_WIDE = 256
Q_LORA = 512
KV_LORA = 512
ROPE_THETA = 10000.0
RWKV_HEADS = 16
RWKV_HEAD = 64
RWKV_WIDTH = RWKV_HEADS * RWKV_HEAD
GN_EPS = 64e-5
N_GROUPS = 4
EXPERTS_PER_GROUP = 8
N_EXPERTS = N_GROUPS * EXPERTS_PER_GROUP
D_EXPERT = 512

Z_RW = 3 * RWKV_WIDTH + 4 * 64 + 128
Z_MLA = 1152
Z_GATE = Z_RW + Z_MLA
Z_TILE = 4352

LANES = 128
CHUNK = 64
META_ROWS = 128
ROW_BLOCK = 256
SC_WIN = 128
SC_WORDS = 256
NEG_BIG = -1e30
MIB = 1024 * 1024


def _params(sem, vmem_mib):
    return pltpu.CompilerParams(dimension_semantics=sem, vmem_limit_bytes=vmem_mib * MIB)


def _dot(a, b):
    return jnp.dot(a, b, preferred_element_type=F32)


def _dot_nt(a, b):
    return lax.dot_general(a, b, (((1,), (1,)), ((), ())), preferred_element_type=F32)


def _dot_tn(a, b):
    return lax.dot_general(a, b, (((0,), (0,)), ((), ())), preferred_element_type=F32)


def _split2(x):
    hi = x.astype(BF16)
    lo = (x - hi.astype(F32)).astype(BF16)
    return hi, lo


def _dot_exact_lhs(a_bf16, x):
    hi, lo = _split2(x)
    return _dot(a_bf16, hi) + _dot(a_bf16, lo)


def _dot_wide(x, w):
    x1, x2 = _split2(x)
    w1, w2 = _split2(w)
    return _dot(x1, w1) + (_dot(x1, w2) + _dot(x2, w1))


def _norm_matmul_kernel(x_ref, g_ref, w_ref, o_ref):
    x = x_ref[...]
    y = x * lax.rsqrt(jnp.mean(x * x, axis=-1, keepdims=True) + RMS_EPS)
    o_ref[...] = _dot((y * g_ref[...]).astype(BF16), w_ref[...])


def _norm_matmul(x, gain, w, tn, tm=512):
    m, d = x.shape
    n = w.shape[1]
    tm = min(tm, m)
    assert m % tm == 0 and n % tn == 0
    return pl.pallas_call(
        _norm_matmul_kernel,
        out_shape=jax.ShapeDtypeStruct((m, n), F32),
        grid=(n // tn, m // tm),
        in_specs=[
            pl.BlockSpec((tm, d), lambda j, i: (i, 0)),
            pl.BlockSpec((1, d), lambda j, i: (0, 0)),
            pl.BlockSpec((d, tn), lambda j, i: (0, j), pipeline_mode=pl.Buffered(1)),
        ],
        out_specs=pl.BlockSpec((tm, tn), lambda j, i: (i, j)),
        compiler_params=_params(("parallel", "parallel"), 56),
        name="norm_matmul",
    )(x, gain.reshape(1, d).astype(F32), w)


def _mla_prep_kernel(z_ref, cos_ref, sin_ref, qa_ref, kva_ref, wq_ref, wkv_ref, qn_ref, kn_ref,
                     q_ref, k_ref, v_ref):
    z = z_ref[...]
    cq = z[:, :Q_LORA]
    ckv = z[:, Q_LORA:Q_LORA + KV_LORA]
    kr = z[:, Q_LORA + KV_LORA:Q_LORA + KV_LORA + QK_ROPE]
    cos = cos_ref[...]
    sin = sin_ref[...]

    def rms(x, g):
        return x * lax.rsqrt(jnp.mean(x * x, axis=-1, keepdims=True) + RMS_EPS) * g

    def rope(x):
        swapped = jnp.concatenate([x[:, QK_ROPE // 2:], x[:, :QK_ROPE // 2]], axis=-1)
        return x * cos + swapped * sin

    q_all = _dot(rms(cq, qa_ref[...]).astype(BF16), wq_ref[...])
    kv_all = _dot(rms(ckv, kva_ref[...]).astype(BF16), wkv_ref[...])
    qn = qn_ref[...]
    kn = kn_ref[...]
    scale = QK_HEAD ** -0.5 * math.log2(math.e)
    ones_col = (lax.broadcasted_iota(jnp.int32, (z.shape[0], V_WIDE - V_HEAD), 1) == 0).astype(BF16)
    kr_sq = jnp.sum(kr * kr, axis=-1, keepdims=True)
    kr_rot = rope(kr * kn[:, QK_NOPE:])
    for h in range(MLA_HEADS):
        qh = q_all[:, h * QK_HEAD:(h + 1) * QK_HEAD]
        qh = qh * (lax.rsqrt(jnp.mean(qh * qh, axis=-1, keepdims=True) + RMS_EPS) * scale) * qn
        q_ref[0, h, :, :QK_NOPE] = qh[:, :QK_NOPE].astype(BF16)
        q_ref[0, h, :, QK_NOPE:] = rope(qh[:, QK_NOPE:]).astype(BF16)
        base = h * (QK_NOPE + V_HEAD)
        k_nope = kv_all[:, base:base + QK_NOPE]
        ms = (jnp.sum(k_nope * k_nope, axis=-1, keepdims=True) + kr_sq) * (1.0 / QK_HEAD)
        inv = lax.rsqrt(ms + RMS_EPS)
        k_ref[0, h, :, :QK_NOPE] = (k_nope * inv * kn[:, :QK_NOPE]).astype(BF16)
        k_ref[0, h, :, QK_NOPE:] = (kr_rot * inv).astype(BF16)
        v_ref[0, h, :, :V_HEAD] = kv_all[:, base + QK_NOPE:base + QK_NOPE + V_HEAD].astype(BF16)
        v_ref[0, h, :, V_HEAD:] = ones_col


def _mla_prep(z, batch, seq, cos, sin, qa, kva, wq, wkv, qn, kn, tm=512):
    tm = min(tm, seq)
    assert seq % tm == 0 and Z_RW % Z_MLA == 0
    nt = seq // tm
    full = lambda b, i: (0, 0)
    return pl.pallas_call(
        _mla_prep_kernel,
        out_shape=(
            jax.ShapeDtypeStruct((batch, MLA_HEADS, seq, QK_HEAD), BF16),
            jax.ShapeDtypeStruct((batch, MLA_HEADS, seq, QK_HEAD), BF16),
            jax.ShapeDtypeStruct((batch, MLA_HEADS, seq, V_WIDE), BF16),
        ),
        grid=(batch, nt),
        in_specs=[
            pl.BlockSpec((tm, Z_MLA), lambda b, i: (b * nt + i, Z_RW // Z_MLA)),
            pl.BlockSpec((tm, QK_ROPE), lambda b, i: (i, 0)),
            pl.BlockSpec((tm, QK_ROPE), lambda b, i: (i, 0)),
            pl.BlockSpec((1, Q_LORA), full),
            pl.BlockSpec((1, KV_LORA), full),
            pl.BlockSpec(wq.shape, full),
            pl.BlockSpec(wkv.shape, full),
            pl.BlockSpec((1, QK_HEAD), full),
            pl.BlockSpec((1, QK_HEAD), full),
        ],
        out_specs=(
            pl.BlockSpec((1, MLA_HEADS, tm, QK_HEAD), lambda b, i: (b, 0, i, 0)),
            pl.BlockSpec((1, MLA_HEADS, tm, QK_HEAD), lambda b, i: (b, 0, i, 0)),
            pl.BlockSpec((1, MLA_HEADS, tm, V_WIDE), lambda b, i: (b, 0, i, 0)),
        ),
        compiler_params=_params(("parallel", "parallel"), 48),
        name="mla_prep",
    )(z, cos, sin, qa, kva, wq, wkv, qn, kn)


def _attention_kernel(q_ref, k_ref, v_ref, km_ref, vm_ref, o_ref, *, tk, n_split):
    q = q_ref[0, 0]
    tq = q.shape[0]
    seq = k_ref.shape[2]

    rows = tq // n_split
    qs = [q[g * rows:(g + 1) * rows] for g in range(n_split)]
    ss = [_dot_nt(x, km_ref[0]) for x in qs]
    lane = lax.broadcasted_iota(jnp.int32, ss[0].shape, 1)
    ss = [jnp.where(lane < N_META, s, NEG_BIG) for s in ss]
    ms = [jnp.max(s, axis=-1, keepdims=True) for s in ss]
    accs = [_dot(jnp.exp2(s - m).astype(BF16), vm_ref[0]) for s, m in zip(ss, ms)]

    def body(c, carry):
        ms, accs = carry
        off = pl.multiple_of(c * tk, tk)
        k = k_ref[0, 0, pl.ds(off, tk), :]
        v = v_ref[0, 0, pl.ds(off, tk), :]
        ss = [_dot_nt(x, k) for x in qs]
        m_new = [jnp.maximum(m, jnp.max(s, axis=-1, keepdims=True)) for m, s in zip(ms, ss)]
        ps = [jnp.exp2(s - m).astype(BF16) for s, m in zip(ss, m_new)]
        accs = [jnp.exp2(m - mn) * a + _dot(p, v) for m, mn, a, p in zip(ms, m_new, accs, ps)]
        return m_new, accs

    ms, accs = lax.fori_loop(0, seq // tk, body, (ms, accs))
    for g in range(n_split):
        acc = accs[g]
        o_ref[0, g * rows:(g + 1) * rows, :] = (
            acc[:, :V_HEAD] / acc[:, V_HEAD:V_HEAD + 1]).astype(BF16)


def _attention(q, k, v, k_meta, v_meta, tq=2048, tk=2048, n_split=4):
    batch, heads, seq, _ = q.shape
    tq = min(tq, seq)
    tk = min(tk, seq)
    assert seq % tq == 0 and seq % tk == 0 and tq % (16 * n_split) == 0
    return pl.pallas_call(
        functools.partial(_attention_kernel, tk=tk, n_split=n_split),
        out_shape=jax.ShapeDtypeStruct((batch, seq, heads * V_HEAD), BF16),
        grid=(batch, heads, seq // tq),
        in_specs=[
            pl.BlockSpec((1, 1, tq, QK_HEAD), lambda b, h, i: (b, h, i, 0)),
            pl.BlockSpec((1, 1, seq, QK_HEAD), lambda b, h, i: (b, h, 0, 0)),
            pl.BlockSpec((1, 1, seq, V_WIDE), lambda b, h, i: (b, h, 0, 0)),
            pl.BlockSpec((1, META_ROWS, QK_HEAD), lambda b, h, i: (h, 0, 0)),
            pl.BlockSpec((1, META_ROWS, V_WIDE), lambda b, h, i: (h, 0, 0)),
        ],
        out_specs=pl.BlockSpec((1, tq, V_HEAD), lambda b, h, i: (b, i, h)),
        compiler_params=_params(("parallel", "parallel", "arbitrary"), 48),
        name="mla_attention",
    )(q, k, v, k_meta, v_meta)


def _head_sum(x, ones_blk):
    xb = x.astype(BF16)
    cols = [_dot(xb[:, c * LANES:(c + 1) * LANES], ones_blk) for c in range(x.shape[1] // LANES)]
    return jnp.concatenate(cols, axis=-1)


def _block_diag(x, bd_mask):
    return jnp.where(bd_mask, jnp.concatenate([x, x], axis=0), jnp.zeros((), x.dtype))


def _tri_inverse(a_list, row, col, bd_mask):
    def same_block(n):
        return (row // n) == (col // n)

    eye = (row == col).astype(F32)
    apow = [jnp.where(same_block(16), a, 0.0) for a in a_list]
    t = [eye - a for a in apow]
    for _ in range(3):
        apow_b = [a.astype(BF16) for a in apow]
        apow = [_dot(a, _block_diag(a, bd_mask)) for a in apow_b]
        t = [ti + _dot(ti.astype(BF16), _block_diag(a.astype(BF16), bd_mask))
             for ti, a in zip(t, apow)]
    n = 32
    while n <= CHUNK:
        off_mask = same_block(n) & jnp.logical_not(same_block(n // 2))
        t_bd = [_block_diag(ti.astype(BF16), bd_mask) for ti in t]
        inner = [_dot(jnp.where(off_mask, a, 0.0).astype(BF16), tb).astype(BF16)
                 for a, tb in zip(a_list, t_bd)]
        t = [ti - _dot(ti.astype(BF16), _block_diag(x, bd_mask)) for ti, x in zip(t, inner)]
        n *= 2
    return t


def _rwkv_kernel(*refs, reverse, n_lead, nb):
    if reverse:
        (zs_ref, kkn_ref, yf_ref, bf_ref, w0_ref, w2_ref, a0_ref, a2_ref, ka_ref, rk_ref,
         g2_ref, lnw_ref, lnb_ref, o_ref, st_ref) = refs
    else:
        (z_ref, zp_ref, zn_ref, lead_ref, sp_ref, sn_ref, w0_ref, w2_ref, a0_ref, a2_ref,
         kk_ref, ka_ref, rk_ref, y_ref, bon_ref, zs_ref, kkn_ref, st_ref) = refs
    c = pl.program_id(1)
    nc = pl.num_programs(1)
    phys = (nc - c) if reverse else c

    @pl.when(c == 0)
    def _():
        st_ref[...] = jnp.zeros_like(st_ref)

    w = RWKV_WIDTH
    rows = lax.broadcasted_iota(jnp.int32, (CHUNK, 1), 0)
    lane_r = lax.broadcasted_iota(jnp.int32, (LANES, LANES), 0)
    lane_c = lax.broadcasted_iota(jnp.int32, (LANES, LANES), 1)
    bd_mask = (lane_r // RWKV_HEAD) == (lane_c // RWKV_HEAD)
    ones_blk = bd_mask.astype(BF16)
    row = lax.broadcasted_iota(jnp.int32, (CHUNK, LANES), 0)
    lane = lax.broadcasted_iota(jnp.int32, (CHUNK, LANES), 1)
    col = lane % CHUNK
    first_head = lane < RWKV_HEAD
    if reverse:
        incl = col >= row
        strict = col > row
    else:
        incl = col <= row
        strict = col < row
    tri = incl[:, :CHUNK].astype(BF16)

    def operands(i):
        if reverse:
            zs = zs_ref[i]
            kkn = kkn_ref[i]
            r, k, v = zs[:, :w], zs[:, w:2 * w], zs[:, 2 * w:3 * w]
        else:
            z = jnp.where(phys == 0, lead_ref[...], z_ref[i])
            prev_row = jnp.where(phys == 1, lead_ref[CHUNK - 1:CHUNK, :], zp_ref[i, 7:8, :])
            prev_row = jnp.where(phys > 0, prev_row, 0.0)
            next_row = jnp.where(phys < nc - 1, zn_ref[i, 0:1, :], 0.0)
            z_prev = jnp.where(rows == 0, prev_row, pltpu.roll(z, 1, axis=0))
            z_next = jnp.where(rows == CHUNK - 1, next_row, pltpu.roll(z, CHUNK - 1, axis=0))
            zs = z + sp_ref[...] * (z_prev - z) + sn_ref[...] * (z_next - z)
            zs_ref[i] = zs
            live = jnp.logical_or(phys > 0, rows >= n_lead).astype(F32)
            r = zs[:, :w] * live
            k = zs[:, w:2 * w] * live
            v = zs[:, 2 * w:3 * w] * live
            kk = k * kk_ref[...]
            kkn = kk * jnp.minimum(lax.rsqrt(_head_sum(kk * kk, ones_blk)), 1e12)
            kkn_ref[i] = kkn
        wd = zs[:, 3 * w:3 * w + LANES]
        ad = zs[:, 3 * w + LANES:3 * w + 2 * LANES]
        w_raw = w0_ref[...] + _dot(jnp.tanh(wd).astype(BF16), w2_ref[...])
        lw = (-math.exp(-0.5)) * jax.nn.sigmoid(w_raw)
        icl = jax.nn.sigmoid(a0_ref[...] + _dot(ad.astype(BF16), a2_ref[...]))
        k_rep = k * (1.0 + (icl - 1.0) * ka_ref[...])
        bvec = kkn * icl
        bonus = _head_sum(r * k_rep * rk_ref[...], ones_blk) * v
        l_in = _dot_exact_lhs(tri, lw)
        l_tot = l_in[0:1, :] if reverse else l_in[CHUNK - 1:CHUNK, :]
        e_neg = jnp.exp(-l_in)
        e_rem = jnp.exp(l_tot - l_in)
        ops = {
            "kd": (kkn * jnp.exp(l_in - lw)).astype(BF16), "rd": (r * jnp.exp(l_in)).astype(BF16),
            "bn": (bvec * e_neg).astype(BF16), "kn": (k_rep * e_neg).astype(BF16),
            "bh": (bvec * e_rem).astype(BF16), "kh": (k_rep * e_rem).astype(BF16),
            "v": v.astype(BF16),
        }
        p_tot_t = jnp.transpose(jnp.broadcast_to(jnp.exp(l_tot), (8, w)))
        return ops, p_tot_t, bonus, zs

    per_seq = [operands(i) for i in range(nb)]

    n_pairs = RWKV_HEADS // 2
    items = [(i, p) for i in range(nb) for p in range(n_pairs)]
    bd = lambda x: _block_diag(x, bd_mask)
    diag_blocks = lambda x: jnp.where(first_head, x[:RWKV_HEAD], x[RWKV_HEAD:])
    kd_p, rd_p, bn_p, kn_p, bh_p, kh_p, v_p = (
        [per_seq[i][0][name][:, p * LANES:(p + 1) * LANES] for i, p in items]
        for name in ("kd", "rd", "bn", "kn", "bh", "kh", "v"))
    v_bd = [bd(x) for x in v_p]
    kr_p = [jnp.concatenate([a, b], axis=0) for a, b in zip(kd_p, rd_p)]
    prod_b = [_dot_nt(x, bd(y)) for x, y in zip(kr_p, bn_p)]
    prod_k = [_dot_nt(x, bd(y)) for x, y in zip(kr_p, kn_p)]
    a_ab = [jnp.where(strict, x[:CHUNK], 0.0) for x in prod_b]
    m_rb = [jnp.where(incl, x[CHUNK:], 0.0).astype(BF16) for x in prod_b]
    a_ak = [jnp.where(strict, x[:CHUNK], 0.0).astype(BF16) for x in prod_k]
    m_rk = [jnp.where(incl, x[CHUNK:], 0.0).astype(BF16) for x in prod_k]
    akv = [_dot(a, vv).astype(BF16) for a, vv in zip(a_ak, v_bd)]
    t_inv = [t.astype(BF16) for t in _tri_inverse(a_ab, row, col, bd_mask)]
    wmat = [(-_dot(t, bd(x))).astype(BF16) for t, x in zip(t_inv, kd_p)]
    u0 = [(-_dot(t, bd(x))).astype(BF16) for t, x in zip(t_inv, akv)]
    u0_bd = [bd(x) for x in u0]
    st = [st_ref[i * n_pairs + p] for i, p in items]
    st_bd = [bd(s.astype(BF16)) for s in st]
    q_mat = [(x.astype(F32) + _dot(m, bd(wm))).astype(BF16)
             for x, m, wm in zip(rd_p, m_rb, wmat)]
    ys = [_dot(qm, sb) + _dot(mb, u) + _dot(mk, vv)
          for qm, sb, mb, u, mk, vv in zip(q_mat, st_bd, m_rb, u0_bd, m_rk, v_bd)]
    g_mat = [diag_blocks(_dot_tn(b, wm)).astype(BF16) for b, wm in zip(bh_p, wmat)]
    h_mat = [diag_blocks(_dot_tn(b, u) + _dot_tn(kx, vv))
             for b, u, kx, vv in zip(bh_p, u0, kh_p, v_p)]
    for n, (i, p) in enumerate(items):
        p_tot_t = per_seq[i][1]
        decay = jnp.where(first_head, p_tot_t[2 * p * RWKV_HEAD:(2 * p + 1) * RWKV_HEAD, 0:1],
                          p_tot_t[(2 * p + 1) * RWKV_HEAD:(2 * p + 2) * RWKV_HEAD, 0:1])
        st_ref[i * n_pairs + p] = decay * st[n] + _dot(g_mat[n], st_bd[n]) + h_mat[n]

    for i in range(nb):
        y = jnp.concatenate(ys[i * n_pairs:(i + 1) * n_pairs], axis=-1)
        _, _, bonus, zs = per_seq[i]
        if not reverse:
            y_ref[i] = y
            bon_ref[i] = bonus
        else:
            y = y + yf_ref[i]
            mu = _head_sum(y, ones_blk) * (1.0 / RWKV_HEAD)
            yc = y - mu
            var = _head_sum(yc * yc, ones_blk) * (1.0 / RWKV_HEAD)
            yn = (yc * lax.rsqrt(var + GN_EPS) * lnw_ref[...] + lnb_ref[...]
                  + (bonus + bf_ref[i]))
            gd = zs[:, 3 * w + 2 * LANES:3 * w + 3 * LANES]
            g = _dot(jax.nn.sigmoid(gd).astype(BF16), g2_ref[...])
            o_ref[i] = (yn * g).astype(BF16)


def _rwkv(z_rw, lead, shared, fwd_p, bwd_p, tail, nb=2):
    batch, seq, _ = z_rw.shape
    zc = Z_RW
    nb = min(nb, batch)
    assert batch % nb == 0 and seq % CHUNK == 0
    ncx = seq // CHUNK
    lp = seq + CHUNK
    sub = CHUNK // 8
    w = RWKV_WIDTH
    n_lead = CHUNK - N_META
    full = lambda b, c: (0, 0)
    row_w = pl.BlockSpec((1, w), full)
    mat_w = pl.BlockSpec((LANES, w), full)
    state = pltpu.VMEM((nb * RWKV_HEADS // 2, RWKV_HEAD, 2 * RWKV_HEAD), F32)

    chunk = lambda b, c: (b, c, 0)
    fwd_specs = [
        pl.BlockSpec((nb, CHUNK, zc), lambda b, c: (b, jnp.maximum(c - 1, 0), 0)),
        pl.BlockSpec((nb, 8, zc), lambda b, c: (b, jnp.maximum((c - 1) * sub - 1, 0), 0)),
        pl.BlockSpec((nb, 8, zc), lambda b, c: (b, jnp.minimum(c * sub, ncx * sub - 1), 0)),
        pl.BlockSpec((CHUNK, zc), full),
        pl.BlockSpec((1, zc), full), pl.BlockSpec((1, zc), full),
        row_w, mat_w, row_w, mat_w, row_w, row_w, row_w,
    ]
    fwd_args = [z_rw, z_rw, z_rw, lead, shared["sp"], shared["sn"], fwd_p["w0"], fwd_p["w2"],
                fwd_p["a0"], fwd_p["a2"], shared["k_k"], shared["k_a"], shared["r_k"]]
    y_f, bon_f, zs, kkn = pl.pallas_call(
        functools.partial(_rwkv_kernel, reverse=False, n_lead=n_lead, nb=nb),
        out_shape=(jax.ShapeDtypeStruct((batch, lp, w), F32),
                   jax.ShapeDtypeStruct((batch, lp, w), F32),
                   jax.ShapeDtypeStruct((batch, lp, zc), F32),
                   jax.ShapeDtypeStruct((batch, lp, w), F32)),
        grid=(batch // nb, ncx + 1),
        in_specs=fwd_specs,
        out_specs=(pl.BlockSpec((nb, CHUNK, w), chunk), pl.BlockSpec((nb, CHUNK, w), chunk),
                   pl.BlockSpec((nb, CHUNK, zc), chunk), pl.BlockSpec((nb, CHUNK, w), chunk)),
        scratch_shapes=[state],
        compiler_params=_params(("parallel", "arbitrary"), 48),
        name="rwkv7_fwd",
    )(*fwd_args)

    rchunk = lambda b, c: (b, ncx - c, 0)
    bwd_specs = [
        pl.BlockSpec((nb, CHUNK, zc), rchunk), pl.BlockSpec((nb, CHUNK, w), rchunk),
        pl.BlockSpec((nb, CHUNK, w), rchunk), pl.BlockSpec((nb, CHUNK, w), rchunk),
        row_w, mat_w, row_w, mat_w, row_w, row_w, mat_w, row_w, row_w,
    ]
    bwd_args = [zs, kkn, y_f, bon_f, bwd_p["w0"], bwd_p["w2"], bwd_p["a0"], bwd_p["a2"],
                shared["k_a"], shared["r_k"], tail["g2"], tail["ln_w"], tail["ln_b"]]
    return pl.pallas_call(
        functools.partial(_rwkv_kernel, reverse=True, n_lead=n_lead, nb=nb),
        out_shape=jax.ShapeDtypeStruct((batch, seq, w), BF16),
        grid=(batch // nb, ncx),
        in_specs=bwd_specs,
        out_specs=pl.BlockSpec((nb, CHUNK, w), lambda b, c: (b, ncx - 1 - c, 0)),
        scratch_shapes=[state],
        compiler_params=_params(("parallel", "arbitrary"), 48),
        name="rwkv7_bwd",
    )(*bwd_args)


def _pack_rows(x):
    half = x.shape[1] // 2
    bits = lax.bitcast_convert_type(x.astype(BF16).astype(F32), jnp.uint32)
    words = bits[:, :half] | (bits[:, half:] >> 16)
    return [words[:, s * SC_WORDS:(s + 1) * SC_WORDS] for s in range(half // SC_WORDS)]


def _unpack_rows(slabs):
    words = jnp.concatenate(slabs, axis=1)
    hi = lax.bitcast_convert_type(words & jnp.uint32(0xFFFF0000), F32)
    lo = lax.bitcast_convert_type(words << 16, F32)
    return jnp.concatenate([hi, lo], axis=1)


def _out_router_kernel(*refs, n_gate):
    oa_ref, ob_ref = refs[:2]
    ga_refs = refs[2:2 + n_gate]
    gb_refs = refs[2 + n_gate:2 + 2 * n_gate]
    (x_ref, wa_ref, wb_ref, wo_ref, g_ref, wr_ref, br_ref,
     h_ref, n2_ref, cmb_ref) = refs[2 + 2 * n_gate:]
    gate_a = jnp.concatenate([jax.nn.sigmoid(r[...]) for r in ga_refs], axis=1)
    gate_b = jnp.concatenate([jax.nn.sigmoid(r[...]) for r in gb_refs], axis=1)
    merged = (gate_a * _dot(oa_ref[...], wa_ref[...])
              + gate_b * _dot(ob_ref[...], wb_ref[...])).astype(BF16)
    h = x_ref[...] + _dot(merged, wo_ref[...])
    h_ref[...] = h
    n2 = h * lax.rsqrt(jnp.mean(h * h, axis=-1, keepdims=True) + RMS_EPS) * g_ref[...]
    for s, slab in enumerate(_pack_rows(n2)):
        n2_ref[s] = slab
    logits = _dot_wide(n2, wr_ref[...]) + br_ref[...]
    lane = lax.broadcasted_iota(jnp.int32, logits.shape, 1)
    big = jnp.int32(1 << 20)
    g_mask = (lane >= N_EXPERTS) & (lane < N_EXPERTS + N_GROUPS)
    gl = jnp.where(g_mask, logits, NEG_BIG)
    g_max = jnp.max(gl, axis=-1, keepdims=True)
    pg_top = 1.0 / jnp.sum(jnp.where(g_mask, jnp.exp(gl - g_max), 0.0), axis=-1, keepdims=True)
    gi = jnp.min(jnp.where(gl == g_max, lane, big), axis=-1, keepdims=True) - N_EXPERTS
    e_mask = (lane >= gi * EXPERTS_PER_GROUP) & (lane < (gi + 1) * EXPERTS_PER_GROUP)
    el = jnp.where(e_mask, logits, NEG_BIG)
    m1 = jnp.max(el, axis=-1, keepdims=True)
    i1 = jnp.min(jnp.where(el == m1, lane, big), axis=-1, keepdims=True)
    el2 = jnp.where(lane == i1, NEG_BIG, el)
    m2 = jnp.max(el2, axis=-1, keepdims=True)
    i2 = jnp.min(jnp.where(el2 == m2, lane, big), axis=-1, keepdims=True)
    e21 = jnp.exp(m2 - m1)
    gate1 = pg_top / (1.0 + e21)
    gate2 = pg_top * e21 / (1.0 + e21)
    cmb_ref[...] = jnp.where(lane == i1, gate1, 0.0) + jnp.where(lane == i2, gate2, 0.0)


def _out_router(x, o_a, o_b, z, w_a, w_b, w_out, gain, w_r, b_r, tm=256, tg=512):
    m, d = x.shape
    tm = min(tm, m)
    assert Z_GATE % tg == 0 and d % tg == 0
    n_gate = d // tg
    g0 = Z_GATE // tg
    slabs = d // 2 // SC_WORDS
    row = lambda i: (i, 0)
    once = pl.Buffered(1)
    const = lambda shape: pl.BlockSpec(shape, lambda i: (0, 0), pipeline_mode=once)
    gate_specs = [pl.BlockSpec((tm, tg), functools.partial(lambda i, c: (i, c), c=g0 + j))
                  for j in range(2 * n_gate)]
    return pl.pallas_call(
        functools.partial(_out_router_kernel, n_gate=n_gate),
        out_shape=(jax.ShapeDtypeStruct((m, d), F32),
                   jax.ShapeDtypeStruct((slabs, m, SC_WORDS), jnp.uint32),
                   jax.ShapeDtypeStruct((m, LANES), F32)),
        grid=(m // tm,),
        in_specs=[pl.BlockSpec((tm, o_a.shape[1]), row), pl.BlockSpec((tm, o_b.shape[1]), row)]
        + gate_specs
        + [pl.BlockSpec((tm, d), row), const(w_a.shape), const(w_b.shape), const((d, d)),
           const((1, d)), const((d, LANES)), const((1, LANES))],
        out_specs=(pl.BlockSpec((tm, d), row),
                   pl.BlockSpec((slabs, tm, SC_WORDS), lambda i: (0, i, 0)),
                   pl.BlockSpec((tm, LANES), row)),
        compiler_params=_params(("parallel",), 48),
        name="out_proj_router",
    )(o_a, o_b, *([z] * (2 * n_gate)), x, w_a, w_b, w_out, gain, w_r, b_r)


def _expert_count_kernel(cmb_ref, cnt_ref):
    @pl.when(pl.program_id(0) == 0)
    def _():
        cnt_ref[...] = jnp.zeros_like(cnt_ref)

    sel = (cmb_ref[...] > 0.0).astype(F32)
    cnt_ref[...] += jnp.sum(sel.reshape(-1, 8, LANES), axis=0)


def _expert_counts(cmb, tm=2048):
    m = cmb.shape[0]
    tm = min(tm, m)
    return pl.pallas_call(
        _expert_count_kernel,
        out_shape=jax.ShapeDtypeStruct((8, LANES), F32),
        grid=(m // tm,),
        in_specs=[pl.BlockSpec((tm, LANES), lambda i: (i, 0))],
        out_specs=pl.BlockSpec((8, LANES), lambda i: (0, 0)),
        compiler_params=_params(("arbitrary",), 32),
        name="moe_expert_counts",
    )(cmb)


def _dispatch_index_kernel(cmb_ref, tri_ref, tril_ref, off_ref, dest_ref, gw_ref, base_ref, *, spare):
    @pl.when(pl.program_id(0) == 0)
    def _():
        base_ref[...] = jnp.zeros_like(base_ref)

    tm = cmb_ref.shape[0]
    seg = tri_ref.shape[0]
    base = base_ref[:, 0:1]
    which = lax.broadcasted_iota(jnp.int32, (LANES, seg), 0)
    for s in range(tm // seg):
        cmb_t = jnp.transpose(cmb_ref[s * seg:(s + 1) * seg, :])
        sel = cmb_t > 0.0
        sel_b = sel.astype(BF16)
        pos = _dot(sel_b, tri_ref[...]) + base
        base = pos[:, seg - 1:seg]
        nth = _dot(tril_ref[...], sel_b)
        row = off_ref[...] + pos - 1.0
        gws = []
        for k in range(2):
            pick = sel & (nth == float(k + 1))
            found = jnp.sum(jnp.where(pick, 1.0, 0.0), axis=0, keepdims=True) > 0.0
            dst = jnp.sum(jnp.where(pick, row, 0.0), axis=0, keepdims=True)
            dest_ref[k:k + 1, s * seg:(s + 1) * seg] = jnp.where(found, dst, spare).astype(jnp.int32)
            gws.append(jnp.sum(jnp.where(pick, cmb_t, 0.0), axis=0, keepdims=True))
        gmat = jnp.where(which == 0, gws[0], jnp.where(which == 1, gws[1], 0.0))
        gw_ref[s * seg:(s + 1) * seg, :] = jnp.transpose(gmat)
    base_ref[...] = jnp.broadcast_to(base, base_ref.shape)


def _dispatch_index(cmb, off_col, spare, tm=2048, seg=1024):
    m = cmb.shape[0]
    tm = min(tm, m)
    seg = min(seg, tm)
    idx = jnp.arange(seg)
    tri = (idx[:, None] <= idx[None, :]).astype(BF16)
    lid = jnp.arange(LANES)
    tril = (lid[:, None] >= lid[None, :]).astype(BF16)
    return pl.pallas_call(
        functools.partial(_dispatch_index_kernel, spare=float(spare)),
        out_shape=(jax.ShapeDtypeStruct((2, m), jnp.int32), jax.ShapeDtypeStruct((m, LANES), F32)),
        grid=(m // tm,),
        in_specs=[pl.BlockSpec((tm, LANES), lambda i: (i, 0)),
                  pl.BlockSpec((seg, seg), lambda i: (0, 0)),
                  pl.BlockSpec((LANES, LANES), lambda i: (0, 0)),
                  pl.BlockSpec((LANES, 1), lambda i: (0, 0))],
        out_specs=(pl.BlockSpec((2, tm), lambda i: (0, i)), pl.BlockSpec((tm, LANES), lambda i: (i, 0))),
        scratch_shapes=[pltpu.VMEM((LANES, LANES), F32)],
        compiler_params=_params(("arbitrary",), 32),
        name="moe_dispatch_index",
    )(cmb, tri, tril, off_col)


def _sc_scatter_rows(src, idx, n_dst):
    n_src, words = src.shape
    n_idx = idx.shape[0]
    n_win = n_src // SC_WIN
    mesh = plsc.VectorSubcoreMesh(core_axis_name="c", subcore_axis_name="s")

    @pl.kernel(out_type=jax.ShapeDtypeStruct((n_dst, words), src.dtype), mesh=mesh,
               scratch_types=[])
    def scatter(x_hbm, i_hbm, o_hbm):
        def body(x_vmem, i_vmem):
            pltpu.sync_copy(x_vmem, o_hbm.at[i_vmem.at[0]])

        pltpu.emit_pipeline(
            body, grid=(n_idx // SC_WIN,),
            in_specs=[pl.BlockSpec((SC_WIN, words), index_map=lambda i: (i % n_win, 0)),
                      pl.BlockSpec((1, SC_WIN), index_map=lambda i: (0, i))],
            out_specs=[],
            core_axis_name=("c", "s"), dimension_semantics=(pltpu.PARALLEL,),
        )(x_hbm, i_hbm)

    return scatter(src, idx.reshape(1, n_idx))


def _sc_gather_rows(src, idx):
    words = src.shape[1]
    n_idx = idx.shape[0]
    mesh = plsc.VectorSubcoreMesh(core_axis_name="c", subcore_axis_name="s")

    @pl.kernel(out_type=jax.ShapeDtypeStruct((n_idx, words), src.dtype), mesh=mesh)
    def gather(x_hbm, i_hbm, o_hbm):
        def body(i_vmem, o_vmem):
            pltpu.sync_copy(x_hbm.at[i_vmem.at[0]], o_vmem)

        pltpu.emit_pipeline(
            body, grid=(n_idx // SC_WIN,),
            in_specs=[pl.BlockSpec((1, SC_WIN), index_map=lambda i: (0, i))],
            out_specs=[pl.BlockSpec((SC_WIN, words), index_map=lambda i: (i, 0))],
            core_axis_name=("c", "s"), dimension_semantics=(pltpu.PARALLEL,),
        )(i_hbm, o_hbm)

    return gather(src, idx.reshape(1, n_idx))


def _expert_block_kernel(blk_e_ref, n_used_ref, xs_ref, wg_ref, wu_ref, wd_ref, ys_ref):
    @pl.when(pl.program_id(0) < n_used_ref[0])
    def _():
        xe = _unpack_rows([xs_ref[s] for s in range(xs_ref.shape[0])]).astype(BF16)
        hg = _dot(xe, wg_ref[0])
        hu = _dot(xe, wu_ref[0])
        act = (hg * jax.nn.sigmoid(hg)) * hu
        for s, slab in enumerate(_pack_rows(_dot(act.astype(BF16), wd_ref[0]))):
            ys_ref[s] = slab


def _expert_blocks(blk_e, n_used, xs, w_gate, w_up, w_down):
    slabs, rows, words = xs.shape
    _, d, f = w_gate.shape
    nb = rows // ROW_BLOCK - 1
    packed = pl.BlockSpec((slabs, ROW_BLOCK, words), lambda j, be, nu: (0, j, 0))
    grid_spec = pltpu.PrefetchScalarGridSpec(
        num_scalar_prefetch=2,
        grid=(nb,),
        in_specs=[packed,
                  pl.BlockSpec((1, d, f), lambda j, be, nu: (be[j], 0, 0)),
                  pl.BlockSpec((1, d, f), lambda j, be, nu: (be[j], 0, 0)),
                  pl.BlockSpec((1, f, d), lambda j, be, nu: (be[j], 0, 0))],
        out_specs=packed,
    )
    return pl.pallas_call(
        _expert_block_kernel,
        out_shape=jax.ShapeDtypeStruct(xs.shape, xs.dtype),
        grid_spec=grid_spec,
        compiler_params=_params(("arbitrary",), 48),
        name="moe_expert_blocks",
    )(blk_e, n_used, xs, w_gate, w_up, w_down)


def _combine_kernel(h_ref, y0_ref, y1_ref, gw_ref, o_ref):
    gw = gw_ref[...]
    w0 = gw[:, 0:1]
    w1 = gw[:, 1:2]
    n_slabs = y0_ref.shape[1]
    y0 = jnp.where(w0 > 0.0, w0 * _unpack_rows([y0_ref[0, s] for s in range(n_slabs)]), 0.0)
    y1 = jnp.where(w1 > 0.0, w1 * _unpack_rows([y1_ref[0, s] for s in range(n_slabs)]), 0.0)
    o_ref[...] = h_ref[...] + (y0 + y1)


def _combine(h, y, gw, tm=512):
    m, d = h.shape
    tm = min(tm, m)
    _, slabs, _, words = y.shape
    row = lambda i: (i, 0)
    return pl.pallas_call(
        _combine_kernel,
        out_shape=jax.ShapeDtypeStruct((m, d), F32),
        grid=(m // tm,),
        in_specs=[pl.BlockSpec((tm, d), row),
                  pl.BlockSpec((1, slabs, tm, words), lambda i: (0, 0, i, 0)),
                  pl.BlockSpec((1, slabs, tm, words), lambda i: (1, 0, i, 0)),
                  pl.BlockSpec((tm, LANES), row)],
        out_specs=pl.BlockSpec((tm, d), row),
        compiler_params=_params(("parallel",), 48),
        name="moe_combine",
    )(h, y, y, gw)


def _moe_dispatch(n2p, h, cmb, w_gate, w_up, w_down):
    slabs, m, words = n2p.shape
    cnt = jnp.sum(_expert_counts(cmb), axis=0)[:N_EXPERTS].astype(jnp.int32)
    nblk = (cnt + ROW_BLOCK - 1) // ROW_BLOCK
    cum = jnp.cumsum(nblk)
    nb = 2 * m // ROW_BLOCK + N_EXPERTS
    off = ((cum - nblk) * ROW_BLOCK).astype(F32)
    off_col = jnp.zeros((LANES, 1), F32).at[:N_EXPERTS, 0].set(off)
    blk_e = jnp.sum(cum[None, :] <= jnp.arange(nb, dtype=jnp.int32)[:, None], axis=1)
    blk_e = jnp.minimum(blk_e, N_EXPERTS - 1).astype(jnp.int32)
    n_used = cum[-1:].astype(jnp.int32)
    spare = nb * ROW_BLOCK
    rows = spare + ROW_BLOCK
    dest, gw = _dispatch_index(cmb, off_col, spare)
    idx = (dest[:, None, :] + (jnp.arange(slabs, dtype=jnp.int32) * rows)[None, :, None]).reshape(-1)
    xs = _sc_scatter_rows(n2p.reshape(slabs * m, words), idx, slabs * rows)
    ys = _expert_blocks(blk_e, n_used, xs.reshape(slabs, rows, words), w_gate, w_up, w_down)
    y = _sc_gather_rows(ys.reshape(slabs * rows, words), idx)
    return _combine(h, y.reshape(2, slabs, m, words), gw)


def _prepare_weights(p):
    d = p["w_in"].shape[0]
    splits = [Q_LORA, KV_LORA, QK_ROPE, 3 * RWKV_WIDTH + 4 * 64 + 128, d, d]
    offs = [0]
    for s in splits:
        offs.append(offs[-1] + s)
    w_in = p["w_in"].astype(BF16)
    zeros64 = jnp.zeros((64, RWKV_WIDTH), F32)

    def pad_lo(a):
        return jnp.concatenate([a, zeros64], axis=0).astype(BF16)

    def pad_hi(a):
        return jnp.concatenate([zeros64, a], axis=0).astype(BF16)

    row = lambda a: a.reshape(1, -1).astype(F32)
    wr = jnp.concatenate(
        [jnp.transpose(p["w_router_expert"], (1, 0, 2)).reshape(d, N_EXPERTS), p["w_router_group"],
         jnp.zeros((d, LANES - N_EXPERTS - N_GROUPS), F32)], axis=1)
    br = jnp.concatenate([p["b_router_expert"].reshape(-1), p["b_router_group"],
                          jnp.zeros((LANES - N_EXPERTS - N_GROUPS,), F32)]).reshape(1, LANES)
    return {
        "attn_norm": p["attn_norm"],
        "w_all": jnp.concatenate(
            [w_in[:, offs[3]:offs[4]], w_in[:, offs[0]:offs[3]],
             jnp.zeros((d, Z_MLA - (offs[3] - offs[0])), BF16), w_in[:, offs[4]:offs[6]]], axis=1),
        "qa": row(p["q_a_norm"]), "kva": row(p["kv_a_norm"]),
        "wq": p["w_q_b"].astype(BF16), "wkv": p["w_kv_b"].astype(BF16),
        "qn": row(p["q_norm"]), "kn": row(p["k_norm"]),
        "shared": {"sp": row(p["shift_prev"]), "sn": row(p["shift_next"]), "k_k": row(p["k_k"]),
                   "k_a": row(p["k_a"]), "r_k": row(p["r_k"])},
        "fwd": {"w0": row(p["decay_w0_fwd"]), "w2": pad_lo(p["decay_w2_fwd"]),
                "a0": row(p["icl_a0_fwd"]), "a2": pad_lo(p["icl_a2_fwd"])},
        "bwd": {"w0": row(p["decay_w0_bwd"]), "w2": pad_hi(p["decay_w2_bwd"]),
                "a0": row(p["icl_a0_bwd"]), "a2": pad_hi(p["icl_a2_bwd"])},
        "tail": {"g2": p["gate_w2"].astype(BF16), "ln_w": row(p["ln_x_w"]), "ln_b": row(p["ln_x_b"])},
        "w_a": p["w_branch_mla"].astype(BF16), "w_b": p["w_branch_rwkv"].astype(BF16),
        "w_out": p["w_out"].astype(BF16), "ffn_norm": row(p["ffn_norm"]),
        "wr": wr, "br": br,
        "w_gate": p["w_expert_gate"].astype(BF16), "w_up": p["w_expert_up"].astype(BF16),
        "w_down": p["w_expert_down"].astype(BF16),
    }


def _rope_tables(length):
    pos = jnp.arange(length, dtype=F32)
    inv = ROPE_THETA ** (-jnp.arange(0, QK_ROPE, 2, dtype=F32) / QK_ROPE)
    ang = pos[:, None] * inv[None, :]
    cos, sin = jnp.cos(ang), jnp.sin(ang)
    return jnp.concatenate([cos, cos], axis=-1), jnp.concatenate([-sin, sin], axis=-1)


def _trunk(x, meta_tokens, wts):
    batch, seq, d = x.shape
    m = batch * seq
    x2 = x.reshape(m, d)
    meta = meta_tokens.astype(F32)
    cos, sin = _rope_tables(seq + N_META)

    zm = _norm_matmul(meta, wts["attn_norm"], wts["w_all"], tn=Z_TILE)
    z = _norm_matmul(x2, wts["attn_norm"], wts["w_all"], tn=Z_TILE)

    mla_w = (wts["qa"], wts["kva"], wts["wq"], wts["wkv"], wts["qn"], wts["kn"])
    q, k, v = _mla_prep(z, batch, seq, cos[N_META:], sin[N_META:], *mla_w)
    _, km, vm = _mla_prep(zm, 1, N_META, cos[:N_META], sin[:N_META], *mla_w)
    pad = ((0, 0), (0, META_ROWS - N_META), (0, 0))
    o_a = _attention(q, k, v, jnp.pad(km[0], pad), jnp.pad(vm[0], pad))
    o_a = o_a.reshape(m, MLA_HEADS * V_HEAD)

    lead = jnp.concatenate([jnp.zeros((CHUNK - N_META, Z_RW), F32), zm[:, :Z_RW]], axis=0)
    o_b = _rwkv(z.reshape(batch, seq, -1), lead, wts["shared"], wts["fwd"], wts["bwd"],
                wts["tail"])
    o_b = o_b.reshape(m, RWKV_WIDTH)

    h, n2, cmb = _out_router(x2, o_a, o_b, z, wts["w_a"], wts["w_b"], wts["w_out"],
                             wts["ffn_norm"], wts["wr"], wts["br"])
    out = _moe_dispatch(n2, h, cmb, wts["w_gate"], wts["w_up"], wts["w_down"])
    return out.reshape(batch, seq, d)


def kernel(x_prompt, x_sample, meta_tokens, attn_norm, w_in, q_a_norm, w_q_b, kv_a_norm, w_kv_b, q_norm, k_norm, shift_prev, shift_next, decay_w0_fwd, decay_w2_fwd, decay_w0_bwd, decay_w2_bwd, icl_a0_fwd, icl_a2_fwd, icl_a0_bwd, icl_a2_bwd, gate_w2, k_k, k_a, r_k, ln_x_w, ln_x_b, w_branch_mla, w_branch_rwkv, w_out, ffn_norm, w_router_group, b_router_group, w_router_expert, b_router_expert, w_expert_gate, w_expert_up, w_expert_down):
    p = {
        "attn_norm": attn_norm, "w_in": w_in, "q_a_norm": q_a_norm, "w_q_b": w_q_b,
        "kv_a_norm": kv_a_norm, "w_kv_b": w_kv_b, "q_norm": q_norm, "k_norm": k_norm,
        "shift_prev": shift_prev, "shift_next": shift_next,
        "decay_w0_fwd": decay_w0_fwd, "decay_w2_fwd": decay_w2_fwd,
        "decay_w0_bwd": decay_w0_bwd, "decay_w2_bwd": decay_w2_bwd,
        "icl_a0_fwd": icl_a0_fwd, "icl_a2_fwd": icl_a2_fwd,
        "icl_a0_bwd": icl_a0_bwd, "icl_a2_bwd": icl_a2_bwd,
        "gate_w2": gate_w2, "k_k": k_k, "k_a": k_a, "r_k": r_k, "ln_x_w": ln_x_w, "ln_x_b": ln_x_b,
        "w_branch_mla": w_branch_mla, "w_branch_rwkv": w_branch_rwkv, "w_out": w_out,
        "ffn_norm": ffn_norm, "w_router_group": w_router_group, "b_router_group": b_router_group,
        "w_router_expert": w_router_expert, "b_router_expert": b_router_expert,
        "w_expert_gate": w_expert_gate, "w_expert_up": w_expert_up, "w_expert_down": w_expert_down,
    }
    wts = _prepare_weights({name: val[0] for name, val in p.items()})
    return _trunk(x_prompt, meta_tokens, wts), _trunk(x_sample, meta_tokens, wts)
```

```python
import functools
import math

import jax
import jax.numpy as jnp
from jax import lax
from jax.experimental import pallas as pl
from jax.experimental.pallas import tpu as pltpu
from jax.experimental.pallas import tpu_sc as plsc

F32 = jnp.float32
BF16 = jnp.bfloat16

N_META = 16
RMS_EPS = 1e-6
MLA_HEADS = 8
QK_NOPE = 128
QK_ROPE = 64
QK_HEAD = QK_NOPE + QK_ROPE
V_HEAD = 128
V_WIDE = 256
Q_LORA = 512
KV_LORA = 512
ROPE_THETA = 10000.0
RWKV_HEADS = 16
RWKV_HEAD = 64
RWKV_WIDTH = RWKV_HEADS * RWKV_HEAD
GN_EPS = 64e-5
N_GROUPS = 4
EXPERTS_PER_GROUP = 8
N_EXPERTS = N_GROUPS * EXPERTS_PER_GROUP
D_EXPERT = 512

Z_RW = 3 * RWKV_WIDTH + 4 * 64 + 128
Z_MLA = 1152
Z_GATE = Z_RW + Z_MLA
Z_TILE = 4352

LANES = 128
CHUNK = 64
META_ROWS = 128
ROW_BLOCK = 256
SC_WIN = 128
SC_WORDS = 256
NEG_BIG = -1e30
MIB = 1024 * 1024


def _params(sem, vmem_mib):
    return pltpu.CompilerParams(dimension_semantics=sem, vmem_limit_bytes=vmem_mib * MIB)


def _dot(a, b):
    return jnp.dot(a, b, preferred_element_type=F32)


def _dot_nt(a, b):
    return lax.dot_general(a, b, (((1,), (1,)), ((), ())), preferred_element_type=F32)


def _dot_tn(a, b):
    return lax.dot_general(a, b, (((0,), (0,)), ((), ())), preferred_element_type=F32)


def _split2(x):
    hi = x.astype(BF16)
    lo = (x - hi.astype(F32)).astype(BF16)
    return hi, lo


def _dot_exact_lhs(a_bf16, x):
    hi, lo = _split2(x)
    return _dot(a_bf16, hi) + _dot(a_bf16, lo)


def _dot_wide(x, w):
    x1, x2 = _split2(x)
    w1, w2 = _split2(w)
    return _dot(x1, w1) + (_dot(x1, w2) + _dot(x2, w1))


def _norm_matmul_kernel(x_ref, g_ref, w_ref, o_ref):
    x = x_ref[...]
    y = x * lax.rsqrt(jnp.mean(x * x, axis=-1, keepdims=True) + RMS_EPS)
    o_ref[...] = _dot((y * g_ref[...]).astype(BF16), w_ref[...])


def _norm_matmul(x, gain, w, tn, tm=512):
    m, d = x.shape
    n = w.shape[1]
    tm = min(tm, m)
    assert m % tm == 0 and n % tn == 0
    return pl.pallas_call(
        _norm_matmul_kernel,
        out_shape=jax.ShapeDtypeStruct((m, n), F32),
        grid=(n // tn, m // tm),
        in_specs=[
            pl.BlockSpec((tm, d), lambda j, i: (i, 0)),
            pl.BlockSpec((1, d), lambda j, i: (0, 0)),
            pl.BlockSpec((d, tn), lambda j, i: (0, j), pipeline_mode=pl.Buffered(1)),
        ],
        out_specs=pl.BlockSpec((tm, tn), lambda j, i: (i, j)),
        compiler_params=_params(("parallel", "parallel"), 56),
        name="norm_matmul",
    )(x, gain.reshape(1, d).astype(F32), w)


def _mla_prep_kernel(z_ref, cos_ref, sin_ref, qa_ref, kva_ref, wq_ref, wkv_ref, qn_ref, kn_ref,
                     q_ref, k_ref, v_ref):
    z = z_ref[...]
    cq = z[:, :Q_LORA]
    ckv = z[:, Q_LORA:Q_LORA + KV_LORA]
    kr = z[:, Q_LORA + KV_LORA:Q_LORA + KV_LORA + QK_ROPE]
    cos = cos_ref[...]
    sin = sin_ref[...]

    def rms(x, g):
        return x * lax.rsqrt(jnp.mean(x * x, axis=-1, keepdims=True) + RMS_EPS) * g

    def rope(x):
        swapped = jnp.concatenate([x[:, QK_ROPE // 2:], x[:, :QK_ROPE // 2]], axis=-1)
        return x * cos + swapped * sin

    q_all = _dot(rms(cq, qa_ref[...]).astype(BF16), wq_ref[...])
    kv_all = _dot(rms(ckv, kva_ref[...]).astype(BF16), wkv_ref[...])
    qn = qn_ref[...]
    kn = kn_ref[...]
    scale = QK_HEAD ** -0.5 * math.log2(math.e)
    ones_col = (lax.broadcasted_iota(jnp.int32, (z.shape[0], V_WIDE - V_HEAD), 1) == 0).astype(BF16)
    kr_sq = jnp.sum(kr * kr, axis=-1, keepdims=True)
    kr_rot = rope(kr * kn[:, QK_NOPE:])
    for h in range(MLA_HEADS):
        qh = q_all[:, h * QK_HEAD:(h + 1) * QK_HEAD]
        qh = qh * (lax.rsqrt(jnp.mean(qh * qh, axis=-1, keepdims=True) + RMS_EPS) * scale) * qn
        q_ref[0, h, :, :QK_NOPE] = qh[:, :QK_NOPE].astype(BF16)
        q_ref[0, h, :, QK_NOPE:] = rope(qh[:, QK_NOPE:]).astype(BF16)
        base = h * (QK_NOPE + V_HEAD)
        k_nope = kv_all[:, base:base + QK_NOPE]
        ms = (jnp.sum(k_nope * k_nope, axis=-1, keepdims=True) + kr_sq) * (1.0 / QK_HEAD)
        inv = lax.rsqrt(ms + RMS_EPS)
        k_ref[0, h, :, :QK_NOPE] = (k_nope * inv * kn[:, :QK_NOPE]).astype(BF16)
        k_ref[0, h, :, QK_NOPE:] = (kr_rot * inv).astype(BF16)
        v_ref[0, h, :, :V_HEAD] = kv_all[:, base + QK_NOPE:base + QK_NOPE + V_HEAD].astype(BF16)
        v_ref[0, h, :, V_HEAD:] = ones_col


def _mla_prep(z, batch, seq, cos, sin, qa, kva, wq, wkv, qn, kn, tm=512):
    tm = min(tm, seq)
    assert seq % tm == 0 and Z_RW % Z_MLA == 0
    nt = seq // tm
    full = lambda b, i: (0, 0)
    return pl.pallas_call(
        _mla_prep_kernel,
        out_shape=(
            jax.ShapeDtypeStruct((batch, MLA_HEADS, seq, QK_HEAD), BF16),
            jax.ShapeDtypeStruct((batch, MLA_HEADS, seq, QK_HEAD), BF16),
            jax.ShapeDtypeStruct((batch, MLA_HEADS, seq, V_WIDE), BF16),
        ),
        grid=(batch, nt),
        in_specs=[
            pl.BlockSpec((tm, Z_MLA), lambda b, i: (b * nt + i, Z_RW // Z_MLA)),
            pl.BlockSpec((tm, QK_ROPE), lambda b, i: (i, 0)),
            pl.BlockSpec((tm, QK_ROPE), lambda b, i: (i, 0)),
            pl.BlockSpec((1, Q_LORA), full),
            pl.BlockSpec((1, KV_LORA), full),
            pl.BlockSpec(wq.shape, full),
            pl.BlockSpec(wkv.shape, full),
            pl.BlockSpec((1, QK_HEAD), full),
            pl.BlockSpec((1, QK_HEAD), full),
        ],
        out_specs=(
            pl.BlockSpec((1, MLA_HEADS, tm, QK_HEAD), lambda b, i: (b, 0, i, 0)),
            pl.BlockSpec((1, MLA_HEADS, tm, QK_HEAD), lambda b, i: (b, 0, i, 0)),
            pl.BlockSpec((1, MLA_HEADS, tm, V_WIDE), lambda b, i: (b, 0, i, 0)),
        ),
        compiler_params=_params(("parallel", "parallel"), 48),
        name="mla_prep",
    )(z, cos, sin, qa, kva, wq, wkv, qn, kn)


def _attention_kernel(q_ref, k_ref, v_ref, km_ref, vm_ref, o_ref, *, tk, n_split):
    q = q_ref[0, 0]
    tq = q.shape[0]
    seq = k_ref.shape[2]

    rows = tq // n_split
    qs = [q[g * rows:(g + 1) * rows] for g in range(n_split)]
    ss = [_dot_nt(x, km_ref[0]) for x in qs]
    lane = lax.broadcasted_iota(jnp.int32, ss[0].shape, 1)
    ss = [jnp.where(lane < N_META, s, NEG_BIG) for s in ss]
    ms = [jnp.max(s, axis=-1, keepdims=True) for s in ss]
    accs = [_dot(jnp.exp2(s - m).astype(BF16), vm_ref[0]) for s, m in zip(ss, ms)]

    def body(c, carry):
        ms, accs = carry
        off = pl.multiple_of(c * tk, tk)
        k = k_ref[0, 0, pl.ds(off, tk), :]
        v = v_ref[0, 0, pl.ds(off, tk), :]
        ss = [_dot_nt(x, k) for x in qs]
        m_new = [jnp.maximum(m, jnp.max(s, axis=-1, keepdims=True)) for m, s in zip(ms, ss)]
        ps = [jnp.exp2(s - m).astype(BF16) for s, m in zip(ss, m_new)]
        accs = [jnp.exp2(m - mn) * a + _dot(p, v) for m, mn, a, p in zip(ms, m_new, accs, ps)]
        return m_new, accs

    ms, accs = lax.fori_loop(0, seq // tk, body, (ms, accs), unroll=True)
    for g in range(n_split):
        acc = accs[g]
        o_ref[0, g * rows:(g + 1) * rows, :] = (
            acc[:, :V_HEAD] / acc[:, V_HEAD:V_HEAD + 1]).astype(BF16)


def _attention(q, k, v, k_meta, v_meta, tq=2048, tk=2048, n_split=4):
    batch, heads, seq, _ = q.shape
    tq = min(tq, seq)
    tk = min(tk, seq)
    assert seq % tq == 0 and seq % tk == 0 and tq % (16 * n_split) == 0
    return pl.pallas_call(
        functools.partial(_attention_kernel, tk=tk, n_split=n_split),
        out_shape=jax.ShapeDtypeStruct((batch, seq, heads * V_HEAD), BF16),
        grid=(batch, heads, seq // tq),
        in_specs=[
            pl.BlockSpec((1, 1, tq, QK_HEAD), lambda b, h, i: (b, h, i, 0)),
            pl.BlockSpec((1, 1, seq, QK_HEAD), lambda b, h, i: (b, h, 0, 0)),
            pl.BlockSpec((1, 1, seq, V_WIDE), lambda b, h, i: (b, h, 0, 0)),
            pl.BlockSpec((1, META_ROWS, QK_HEAD), lambda b, h, i: (h, 0, 0)),
            pl.BlockSpec((1, META_ROWS, V_WIDE), lambda b, h, i: (h, 0, 0)),
        ],
        out_specs=pl.BlockSpec((1, tq, V_HEAD), lambda b, h, i: (b, i, h)),
        compiler_params=_params(("parallel", "parallel", "arbitrary"), 48),
        name="mla_attention",
    )(q, k, v, k_meta, v_meta)


def _head_sum(x, ones_blk):
    xb = x.astype(BF16)
    cols = [_dot(xb[:, c * LANES:(c + 1) * LANES], ones_blk) for c in range(x.shape[1] // LANES)]
    return jnp.concatenate(cols, axis=-1)


def _block_diag(x, bd_mask):
    return jnp.where(bd_mask, jnp.concatenate([x, x], axis=0), jnp.zeros((), x.dtype))


def _tri_inverse(a_list, row, col, bd_mask):
    def same_block(n):
        return (row // n) == (col // n)

    eye = (row == col).astype(F32)
    apow = [jnp.where(same_block(16), a, 0.0) for a in a_list]
    t = [eye - a for a in apow]
    for _ in range(3):
        apow_b = [a.astype(BF16) for a in apow]
        apow = [_dot(a, _block_diag(a, bd_mask)) for a in apow_b]
        t = [ti + _dot(ti.astype(BF16), _block_diag(a.astype(BF16), bd_mask))
             for ti, a in zip(t, apow)]
    n = 32
    while n <= CHUNK:
        off_mask = same_block(n) & jnp.logical_not(same_block(n // 2))
        t_bd = [_block_diag(ti.astype(BF16), bd_mask) for ti in t]
        inner = [_dot(jnp.where(off_mask, a, 0.0).astype(BF16), tb).astype(BF16)
                 for a, tb in zip(a_list, t_bd)]
        t = [ti - _dot(ti.astype(BF16), _block_diag(x, bd_mask)) for ti, x in zip(t, inner)]
        n *= 2
    return t


def _rwkv_kernel(*refs, reverse, n_lead, nb):
    if reverse:
        (zs_ref, kkn_ref, yf_ref, bf_ref, w0_ref, w2_ref, a0_ref, a2_ref, ka_ref, rk_ref,
         g2_ref, lnw_ref, lnb_ref, o_ref, st_ref) = refs
    else:
        (z_ref, zp_ref, zn_ref, lead_ref, sp_ref, sn_ref, w0_ref, w2_ref, a0_ref, a2_ref,
         kk_ref, ka_ref, rk_ref, y_ref, bon_ref, zs_ref, kkn_ref, st_ref) = refs
    c = pl.program_id(1)
    nc = pl.num_programs(1)
    phys = (nc - c) if reverse else c

    @pl.when(c == 0)
    def _():
        st_ref[...] = jnp.zeros_like(st_ref)

    w = RWKV_WIDTH
    rows = lax.broadcasted_iota(jnp.int32, (CHUNK, 1), 0)
    lane_r = lax.broadcasted_iota(jnp.int32, (LANES, LANES), 0)
    lane_c = lax.broadcasted_iota(jnp.int32, (LANES, LANES), 1)
    bd_mask = (lane_r // RWKV_HEAD) == (lane_c // RWKV_HEAD)
    ones_blk = bd_mask.astype(BF16)
    row = lax.broadcasted_iota(jnp.int32, (CHUNK, LANES), 0)
    lane = lax.broadcasted_iota(jnp.int32, (CHUNK, LANES), 1)
    col = lane % CHUNK
    first_head = lane < RWKV_HEAD
    if reverse:
        incl = col >= row
        strict = col > row
    else:
        incl = col <= row
        strict = col < row
    tri = incl[:, :CHUNK].astype(BF16)

    def operands(i):
        if reverse:
            zs = zs_ref[i]
            kkn = kkn_ref[i]
            r, k, v = zs[:, :w], zs[:, w:2 * w], zs[:, 2 * w:3 * w]
        else:
            z = jnp.where(phys == 0, lead_ref[...], z_ref[i])
            prev_row = jnp.where(phys == 1, lead_ref[CHUNK - 1:CHUNK, :], zp_ref[i, 7:8, :])
            prev_row = jnp.where(phys > 0, prev_row, 0.0)
            next_row = jnp.where(phys < nc - 1, zn_ref[i, 0:1, :], 0.0)
            z_prev = jnp.where(rows == 0, prev_row, pltpu.roll(z, 1, axis=0))
            z_next = jnp.where(rows == CHUNK - 1, next_row, pltpu.roll(z, CHUNK - 1, axis=0))
            zs = z + sp_ref[...] * (z_prev - z) + sn_ref[...] * (z_next - z)
            zs_ref[i] = zs
            live = jnp.logical_or(phys > 0, rows >= n_lead).astype(F32)
            r = zs[:, :w] * live
            k = zs[:, w:2 * w] * live
            v = zs[:, 2 * w:3 * w] * live
            kk = k * kk_ref[...]
            kkn = kk * jnp.minimum(lax.rsqrt(_head_sum(kk * kk, ones_blk)), 1e12)
            kkn_ref[i] = kkn
        wd = zs[:, 3 * w:3 * w + LANES]
        ad = zs[:, 3 * w + LANES:3 * w + 2 * LANES]
        w_raw = w0_ref[...] + _dot(jnp.tanh(wd).astype(BF16), w2_ref[...])
        lw = (-math.exp(-0.5)) * jax.nn.sigmoid(w_raw)
        icl = jax.nn.sigmoid(a0_ref[...] + _dot(ad.astype(BF16), a2_ref[...]))
        k_rep = k * (1.0 + (icl - 1.0) * ka_ref[...])
        bvec = kkn * icl
        bonus = _head_sum(r * k_rep * rk_ref[...], ones_blk) * v
        l_in = _dot_exact_lhs(tri, lw)
        l_tot = l_in[0:1, :] if reverse else l_in[CHUNK - 1:CHUNK, :]
        e_neg = jnp.exp(-l_in)
        e_rem = jnp.exp(l_tot - l_in)
        ops = {
            "kd": (kkn * jnp.exp(l_in - lw)).astype(BF16), "rd": (r * jnp.exp(l_in)).astype(BF16),
            "bn": (bvec * e_neg).astype(BF16), "kn": (k_rep * e_neg).astype(BF16),
            "bh": (bvec * e_rem).astype(BF16), "kh": (k_rep * e_rem).astype(BF16),
            "v": v.astype(BF16),
        }
        p_tot_t = jnp.transpose(jnp.broadcast_to(jnp.exp(l_tot), (8, w)))
        return ops, p_tot_t, bonus, zs

    per_seq = [operands(i) for i in range(nb)]

    n_pairs = RWKV_HEADS // 2
    items = [(i, p) for i in range(nb) for p in range(n_pairs)]
    bd = lambda x: _block_diag(x, bd_mask)
    diag_blocks = lambda x: jnp.where(first_head, x[:RWKV_HEAD], x[RWKV_HEAD:])
    kd_p, rd_p, bn_p, kn_p, bh_p, kh_p, v_p = (
        [per_seq[i][0][name][:, p * LANES:(p + 1) * LANES] for i, p in items]
        for name in ("kd", "rd", "bn", "kn", "bh", "kh", "v"))
    v_bd = [bd(x) for x in v_p]
    kr_p = [jnp.concatenate([a, b], axis=0) for a, b in zip(kd_p, rd_p)]
    prod_b = [_dot_nt(x, bd(y)) for x, y in zip(kr_p, bn_p)]
    prod_k = [_dot_nt(x, bd(y)) for x, y in zip(kr_p, kn_p)]
    a_ab = [jnp.where(strict, x[:CHUNK], 0.0) for x in prod_b]
    m_rb = [jnp.where(incl, x[CHUNK:], 0.0).astype(BF16) for x in prod_b]
    a_ak = [jnp.where(strict, x[:CHUNK], 0.0).astype(BF16) for x in prod_k]
    m_rk = [jnp.where(incl, x[CHUNK:], 0.0).astype(BF16) for x in prod_k]
    akv = [_dot(a, vv).astype(BF16) for a, vv in zip(a_ak, v_bd)]
    t_inv = [t.astype(BF16) for t in _tri_inverse(a_ab, row, col, bd_mask)]
    wmat = [(-_dot(t, bd(x))).astype(BF16) for t, x in zip(t_inv, kd_p)]
    u0 = [(-_dot(t, bd(x))).astype(BF16) for t, x in zip(t_inv, akv)]
    u0_bd = [bd(x) for x in u0]
    st = [st_ref[i * n_pairs + p] for i, p in items]
    st_bd = [bd(s.astype(BF16)) for s in st]
    q_mat = [(x.astype(F32) + _dot(m, bd(wm))).astype(BF16)
             for x, m, wm in zip(rd_p, m_rb, wmat)]
    ys = [_dot(qm, sb) + _dot(mb, u) + _dot(mk, vv)
          for qm, sb, mb, u, mk, vv in zip(q_mat, st_bd, m_rb, u0_bd, m_rk, v_bd)]
    g_mat = [diag_blocks(_dot_tn(b, wm)).astype(BF16) for b, wm in zip(bh_p, wmat)]
    h_mat = [diag_blocks(_dot_tn(b, u) + _dot_tn(kx, vv))
             for b, u, kx, vv in zip(bh_p, u0, kh_p, v_p)]
    for n, (i, p) in enumerate(items):
        p_tot_t = per_seq[i][1]
        decay = jnp.where(first_head, p_tot_t[2 * p * RWKV_HEAD:(2 * p + 1) * RWKV_HEAD, 0:1],
                          p_tot_t[(2 * p + 1) * RWKV_HEAD:(2 * p + 2) * RWKV_HEAD, 0:1])
        st_ref[i * n_pairs + p] = decay * st[n] + _dot(g_mat[n], st_bd[n]) + h_mat[n]

    for i in range(nb):
        y = jnp.concatenate(ys[i * n_pairs:(i + 1) * n_pairs], axis=-1)
        _, _, bonus, zs = per_seq[i]
        if not reverse:
            y_ref[i] = y
            bon_ref[i] = bonus
        else:
            y = y + yf_ref[i]
            mu = _head_sum(y, ones_blk) * (1.0 / RWKV_HEAD)
            yc = y - mu
            var = _head_sum(yc * yc, ones_blk) * (1.0 / RWKV_HEAD)
            yn = (yc * lax.rsqrt(var + GN_EPS) * lnw_ref[...] + lnb_ref[...]
                  + (bonus + bf_ref[i]))
            gd = zs[:, 3 * w + 2 * LANES:3 * w + 3 * LANES]
            g = _dot(jax.nn.sigmoid(gd).astype(BF16), g2_ref[...])
            o_ref[i] = (yn * g).astype(BF16)


def _rwkv(z_rw, lead, shared, fwd_p, bwd_p, tail, nb=2):
    batch, seq, _ = z_rw.shape
    zc = Z_RW
    nb = min(nb, batch)
    assert batch % nb == 0 and seq % CHUNK == 0
    ncx = seq // CHUNK
    lp = seq + CHUNK
    sub = CHUNK // 8
    w = RWKV_WIDTH
    n_lead = CHUNK - N_META
    full = lambda b, c: (0, 0)
    row_w = pl.BlockSpec((1, w), full)
    mat_w = pl.BlockSpec((LANES, w), full)
    state = pltpu.VMEM((nb * RWKV_HEADS // 2, RWKV_HEAD, 2 * RWKV_HEAD), F32)

    chunk = lambda b, c: (b, c, 0)
    fwd_specs = [
        pl.BlockSpec((nb, CHUNK, zc), lambda b, c: (b, jnp.maximum(c - 1, 0), 0)),
        pl.BlockSpec((nb, 8, zc), lambda b, c: (b, jnp.maximum((c - 1) * sub - 1, 0), 0)),
        pl.BlockSpec((nb, 8, zc), lambda b, c: (b, jnp.minimum(c * sub, ncx * sub - 1), 0)),
        pl.BlockSpec((CHUNK, zc), full),
        pl.BlockSpec((1, zc), full), pl.BlockSpec((1, zc), full),
        row_w, mat_w, row_w, mat_w, row_w, row_w, row_w,
    ]
    fwd_args = [z_rw, z_rw, z_rw, lead, shared["sp"], shared["sn"], fwd_p["w0"], fwd_p["w2"],
                fwd_p["a0"], fwd_p["a2"], shared["k_k"], shared["k_a"], shared["r_k"]]
    y_f, bon_f, zs, kkn = pl.pallas_call(
        functools.partial(_rwkv_kernel, reverse=False, n_lead=n_lead, nb=nb),
        out_shape=(jax.ShapeDtypeStruct((batch, lp, w), F32),
                   jax.ShapeDtypeStruct((batch, lp, w), F32),
                   jax.ShapeDtypeStruct((batch, lp, zc), F32),
                   jax.ShapeDtypeStruct((batch, lp, w), F32)),
        grid=(batch // nb, ncx + 1),
        in_specs=fwd_specs,
        out_specs=(pl.BlockSpec((nb, CHUNK, w), chunk), pl.BlockSpec((nb, CHUNK, w), chunk),
                   pl.BlockSpec((nb, CHUNK, zc), chunk), pl.BlockSpec((nb, CHUNK, w), chunk)),
        scratch_shapes=[state],
        compiler_params=_params(("parallel", "arbitrary"), 48),
        name="rwkv7_fwd",
    )(*fwd_args)

    rchunk = lambda b, c: (b, ncx - c, 0)
    bwd_specs = [
        pl.BlockSpec((nb, CHUNK, zc), rchunk), pl.BlockSpec((nb, CHUNK, w), rchunk),
        pl.BlockSpec((nb, CHUNK, w), rchunk), pl.BlockSpec((nb, CHUNK, w), rchunk),
        row_w, mat_w, row_w, mat_w, row_w, row_w, mat_w, row_w, row_w,
    ]
    bwd_args = [zs, kkn, y_f, bon_f, bwd_p["w0"], bwd_p["w2"], bwd_p["a0"], bwd_p["a2"],
                shared["k_a"], shared["r_k"], tail["g2"], tail["ln_w"], tail["ln_b"]]
    return pl.pallas_call(
        functools.partial(_rwkv_kernel, reverse=True, n_lead=n_lead, nb=nb),
        out_shape=jax.ShapeDtypeStruct((batch, seq, w), BF16),
        grid=(batch // nb, ncx),
        in_specs=bwd_specs,
        out_specs=pl.BlockSpec((nb, CHUNK, w), lambda b, c: (b, ncx - 1 - c, 0)),
        scratch_shapes=[state],
        compiler_params=_params(("parallel", "arbitrary"), 48),
        name="rwkv7_bwd",
    )(*bwd_args)


def _pack_rows(x):
    half = x.shape[1] // 2
    bits = lax.bitcast_convert_type(x.astype(BF16).astype(F32), jnp.uint32)
    words = bits[:, :half] | (bits[:, half:] >> 16)
    return [words[:, s * SC_WORDS:(s + 1) * SC_WORDS] for s in range(half // SC_WORDS)]


def _unpack_rows(slabs):
    words = jnp.concatenate(slabs, axis=1)
    hi = lax.bitcast_convert_type(words & jnp.uint32(0xFFFF0000), F32)
    lo = lax.bitcast_convert_type(words << 16, F32)
    return jnp.concatenate([hi, lo], axis=1)


def _out_router_kernel(*refs, n_gate):
    oa_ref, ob_ref = refs[:2]
    ga_refs = refs[2:2 + n_gate]
    gb_refs = refs[2 + n_gate:2 + 2 * n_gate]
    (x_ref, wa_ref, wb_ref, wo_ref, g_ref, wr_ref, br_ref,
     h_ref, n2_ref, cmb_ref) = refs[2 + 2 * n_gate:]
    gate_a = jnp.concatenate([jax.nn.sigmoid(r[...]) for r in ga_refs], axis=1)
    gate_b = jnp.concatenate([jax.nn.sigmoid(r[...]) for r in gb_refs], axis=1)
    merged = (gate_a * _dot(oa_ref[...], wa_ref[...])
              + gate_b * _dot(ob_ref[...], wb_ref[...])).astype(BF16)
    h = x_ref[...] + _dot(merged, wo_ref[...])
    h_ref[...] = h
    n2 = h * lax.rsqrt(jnp.mean(h * h, axis=-1, keepdims=True) + RMS_EPS) * g_ref[...]
    for s, slab in enumerate(_pack_rows(n2)):
        n2_ref[s] = slab
    logits = _dot_wide(n2, wr_ref[...]) + br_ref[...]
    lane = lax.broadcasted_iota(jnp.int32, logits.shape, 1)
    big = jnp.int32(1 << 20)
    g_mask = (lane >= N_EXPERTS) & (lane < N_EXPERTS + N_GROUPS)
    gl = jnp.where(g_mask, logits, NEG_BIG)
    g_max = jnp.max(gl, axis=-1, keepdims=True)
    pg_top = 1.0 / jnp.sum(jnp.where(g_mask, jnp.exp(gl - g_max), 0.0), axis=-1, keepdims=True)
    gi = jnp.min(jnp.where(gl == g_max, lane, big), axis=-1, keepdims=True) - N_EXPERTS
    e_mask = (lane >= gi * EXPERTS_PER_GROUP) & (lane < (gi + 1) * EXPERTS_PER_GROUP)
    el = jnp.where(e_mask, logits, NEG_BIG)
    m1 = jnp.max(el, axis=-1, keepdims=True)
    i1 = jnp.min(jnp.where(el == m1, lane, big), axis=-1, keepdims=True)
    el2 = jnp.where(lane == i1, NEG_BIG, el)
    m2 = jnp.max(el2, axis=-1, keepdims=True)
    i2 = jnp.min(jnp.where(el2 == m2, lane, big), axis=-1, keepdims=True)
    e21 = jnp.exp(m2 - m1)
    gate1 = pg_top / (1.0 + e21)
    gate2 = pg_top * e21 / (1.0 + e21)
    cmb_ref[...] = jnp.where(lane == i1, gate1, 0.0) + jnp.where(lane == i2, gate2, 0.0)


def _out_router(x, o_a, o_b, z, w_a, w_b, w_out, gain, w_r, b_r, tm=256, tg=512):
    m, d = x.shape
    tm = min(tm, m)
    assert Z_GATE % tg == 0 and d % tg == 0
    n_gate = d // tg
    g0 = Z_GATE // tg
    slabs = d // 2 // SC_WORDS
    row = lambda i: (i, 0)
    once = pl.Buffered(1)
    const = lambda shape: pl.BlockSpec(shape, lambda i: (0, 0), pipeline_mode=once)
    gate_specs = [pl.BlockSpec((tm, tg), functools.partial(lambda i, c: (i, c), c=g0 + j))
                  for j in range(2 * n_gate)]
    return pl.pallas_call(
        functools.partial(_out_router_kernel, n_gate=n_gate),
        out_shape=(jax.ShapeDtypeStruct((m, d), F32),
                   jax.ShapeDtypeStruct((slabs, m, SC_WORDS), jnp.uint32),
                   jax.ShapeDtypeStruct((m, LANES), F32)),
        grid=(m // tm,),
        in_specs=[pl.BlockSpec((tm, o_a.shape[1]), row), pl.BlockSpec((tm, o_b.shape[1]), row)]
        + gate_specs
        + [pl.BlockSpec((tm, d), row), const(w_a.shape), const(w_b.shape), const((d, d)),
           const((1, d)), const((d, LANES)), const((1, LANES))],
        out_specs=(pl.BlockSpec((tm, d), row),
                   pl.BlockSpec((slabs, tm, SC_WORDS), lambda i: (0, i, 0)),
                   pl.BlockSpec((tm, LANES), row)),
        compiler_params=_params(("parallel",), 48),
        name="out_proj_router",
    )(o_a, o_b, *([z] * (2 * n_gate)), x, w_a, w_b, w_out, gain, w_r, b_r)


def _expert_count_kernel(cmb_ref, cnt_ref):
    @pl.when(pl.program_id(0) == 0)
    def _():
        cnt_ref[...] = jnp.zeros_like(cnt_ref)

    sel = (cmb_ref[...] > 0.0).astype(F32)
    cnt_ref[...] += jnp.sum(sel.reshape(-1, 8, LANES), axis=0)


def _expert_counts(cmb, tm=2048):
    m = cmb.shape[0]
    tm = min(tm, m)
    return pl.pallas_call(
        _expert_count_kernel,
        out_shape=jax.ShapeDtypeStruct((8, LANES), F32),
        grid=(m // tm,),
        in_specs=[pl.BlockSpec((tm, LANES), lambda i: (i, 0))],
        out_specs=pl.BlockSpec((8, LANES), lambda i: (0, 0)),
        compiler_params=_params(("arbitrary",), 32),
        name="moe_expert_counts",
    )(cmb)


def _dispatch_index_kernel(cmb_ref, tri_ref, tril_ref, off_ref, dest_ref, gw_ref, base_ref, *, spare):
    @pl.when(pl.program_id(0) == 0)
    def _():
        base_ref[...] = jnp.zeros_like(base_ref)

    tm = cmb_ref.shape[0]
    seg = tri_ref.shape[0]
    base = base_ref[:, 0:1]
    which = lax.broadcasted_iota(jnp.int32, (LANES, seg), 0)
    for s in range(tm // seg):
        cmb_t = jnp.transpose(cmb_ref[s * seg:(s + 1) * seg, :])
        sel = cmb_t > 0.0
        sel_b = sel.astype(BF16)
        pos = _dot(sel_b, tri_ref[...]) + base
        base = pos[:, seg - 1:seg]
        nth = _dot(tril_ref[...], sel_b)
        row = off_ref[...] + pos - 1.0
        gws = []
        for k in range(2):
            pick = sel & (nth == float(k + 1))
            found = jnp.sum(jnp.where(pick, 1.0, 0.0), axis=0, keepdims=True) > 0.0
            dst = jnp.sum(jnp.where(pick, row, 0.0), axis=0, keepdims=True)
            dest_ref[k:k + 1, s * seg:(s + 1) * seg] = jnp.where(found, dst, spare).astype(jnp.int32)
            gws.append(jnp.sum(jnp.where(pick, cmb_t, 0.0), axis=0, keepdims=True))
        gmat = jnp.where(which == 0, gws[0], jnp.where(which == 1, gws[1], 0.0))
        gw_ref[s * seg:(s + 1) * seg, :] = jnp.transpose(gmat)
    base_ref[...] = jnp.broadcast_to(base, base_ref.shape)


def _dispatch_index(cmb, off_col, spare, tm=2048, seg=1024):
    m = cmb.shape[0]
    tm = min(tm, m)
    seg = min(seg, tm)
    idx = jnp.arange(seg)
    tri = (idx[:, None] <= idx[None, :]).astype(BF16)
    lid = jnp.arange(LANES)
    tril = (lid[:, None] >= lid[None, :]).astype(BF16)
    return pl.pallas_call(
        functools.partial(_dispatch_index_kernel, spare=float(spare)),
        out_shape=(jax.ShapeDtypeStruct((2, m), jnp.int32), jax.ShapeDtypeStruct((m, LANES), F32)),
        grid=(m // tm,),
        in_specs=[pl.BlockSpec((tm, LANES), lambda i: (i, 0)),
                  pl.BlockSpec((seg, seg), lambda i: (0, 0)),
                  pl.BlockSpec((LANES, LANES), lambda i: (0, 0)),
                  pl.BlockSpec((LANES, 1), lambda i: (0, 0))],
        out_specs=(pl.BlockSpec((2, tm), lambda i: (0, i)), pl.BlockSpec((tm, LANES), lambda i: (i, 0))),
        scratch_shapes=[pltpu.VMEM((LANES, LANES), F32)],
        compiler_params=_params(("arbitrary",), 32),
        name="moe_dispatch_index",
    )(cmb, tri, tril, off_col)


def _sc_scatter_rows(src, idx, n_dst):
    n_src, words = src.shape
    n_idx = idx.shape[0]
    n_win = n_src // SC_WIN
    mesh = plsc.VectorSubcoreMesh(core_axis_name="c", subcore_axis_name="s")

    @pl.kernel(out_type=jax.ShapeDtypeStruct((n_dst, words), src.dtype), mesh=mesh,
               scratch_types=[])
    def scatter(x_hbm, i_hbm, o_hbm):
        def body(x_vmem, i_vmem):
            pltpu.sync_copy(x_vmem, o_hbm.at[i_vmem.at[0]])

        pltpu.emit_pipeline(
            body, grid=(n_idx // SC_WIN,),
            in_specs=[pl.BlockSpec((SC_WIN, words), index_map=lambda i: (i % n_win, 0)),
                      pl.BlockSpec((1, SC_WIN), index_map=lambda i: (0, i))],
            out_specs=[],
            core_axis_name=("c", "s"), dimension_semantics=(pltpu.PARALLEL,),
        )(x_hbm, i_hbm)

    return scatter(src, idx.reshape(1, n_idx))


def _sc_gather_rows(src, idx):
    words = src.shape[1]
    n_idx = idx.shape[0]
    mesh = plsc.VectorSubcoreMesh(core_axis_name="c", subcore_axis_name="s")

    @pl.kernel(out_type=jax.ShapeDtypeStruct((n_idx, words), src.dtype), mesh=mesh)
    def gather(x_hbm, i_hbm, o_hbm):
        def body(i_vmem, o_vmem):
            pltpu.sync_copy(x_hbm.at[i_vmem.at[0]], o_vmem)

        pltpu.emit_pipeline(
            body, grid=(n_idx // SC_WIN,),
            in_specs=[pl.BlockSpec((1, SC_WIN), index_map=lambda i: (0, i))],
            out_specs=[pl.BlockSpec((SC_WIN, words), index_map=lambda i: (i, 0))],
            core_axis_name=("c", "s"), dimension_semantics=(pltpu.PARALLEL,),
        )(i_hbm, o_hbm)

    return gather(src, idx.reshape(1, n_idx))


def _expert_block_kernel(blk_e_ref, n_used_ref, xs_ref, wg_ref, wu_ref, wd_ref, ys_ref):
    @pl.when(pl.program_id(0) < n_used_ref[0])
    def _():
        xe = _unpack_rows([xs_ref[s] for s in range(xs_ref.shape[0])]).astype(BF16)
        hg = _dot(xe, wg_ref[0])
        hu = _dot(xe, wu_ref[0])
        act = (hg * jax.nn.sigmoid(hg)) * hu
        for s, slab in enumerate(_pack_rows(_dot(act.astype(BF16), wd_ref[0]))):
            ys_ref[s] = slab


def _expert_blocks(blk_e, n_used, xs, w_gate, w_up, w_down):
    slabs, rows, words = xs.shape
    _, d, f = w_gate.shape
    nb = rows // ROW_BLOCK - 1
    packed = pl.BlockSpec((slabs, ROW_BLOCK, words), lambda j, be, nu: (0, j, 0))
    grid_spec = pltpu.PrefetchScalarGridSpec(
        num_scalar_prefetch=2,
        grid=(nb,),
        in_specs=[packed,
                  pl.BlockSpec((1, d, f), lambda j, be, nu: (be[j], 0, 0)),
                  pl.BlockSpec((1, d, f), lambda j, be, nu: (be[j], 0, 0)),
                  pl.BlockSpec((1, f, d), lambda j, be, nu: (be[j], 0, 0))],
        out_specs=packed,
    )
    return pl.pallas_call(
        _expert_block_kernel,
        out_shape=jax.ShapeDtypeStruct(xs.shape, xs.dtype),
        grid_spec=grid_spec,
        compiler_params=_params(("arbitrary",), 48),
        name="moe_expert_blocks",
    )(blk_e, n_used, xs, w_gate, w_up, w_down)


def _combine_kernel(h_ref, y0_ref, y1_ref, gw_ref, o_ref):
    gw = gw_ref[...]
    w0 = gw[:, 0:1]
    w1 = gw[:, 1:2]
    n_slabs = y0_ref.shape[1]
    y0 = jnp.where(w0 > 0.0, w0 * _unpack_rows([y0_ref[0, s] for s in range(n_slabs)]), 0.0)
    y1 = jnp.where(w1 > 0.0, w1 * _unpack_rows([y1_ref[0, s] for s in range(n_slabs)]), 0.0)
    o_ref[...] = h_ref[...] + (y0 + y1)


def _combine(h, y, gw, tm=512):
    m, d = h.shape
    tm = min(tm, m)
    _, slabs, _, words = y.shape
    row = lambda i: (i, 0)
    return pl.pallas_call(
        _combine_kernel,
        out_shape=jax.ShapeDtypeStruct((m, d), F32),
        grid=(m // tm,),
        in_specs=[pl.BlockSpec((tm, d), row),
                  pl.BlockSpec((1, slabs, tm, words), lambda i: (0, 0, i, 0)),
                  pl.BlockSpec((1, slabs, tm, words), lambda i: (1, 0, i, 0)),
                  pl.BlockSpec((tm, LANES), row)],
        out_specs=pl.BlockSpec((tm, d), row),
        compiler_params=_params(("parallel",), 48),
        name="moe_combine",
    )(h, y, y, gw)


def _moe_dispatch(n2p, h, cmb, w_gate, w_up, w_down):
    slabs, m, words = n2p.shape
    cnt = jnp.sum(_expert_counts(cmb), axis=0)[:N_EXPERTS].astype(jnp.int32)
    nblk = (cnt + ROW_BLOCK - 1) // ROW_BLOCK
    cum = jnp.cumsum(nblk)
    nb = 2 * m // ROW_BLOCK + N_EXPERTS
    off = ((cum - nblk) * ROW_BLOCK).astype(F32)
    off_col = jnp.zeros((LANES, 1), F32).at[:N_EXPERTS, 0].set(off)
    blk_e = jnp.sum(cum[None, :] <= jnp.arange(nb, dtype=jnp.int32)[:, None], axis=1)
    blk_e = jnp.minimum(blk_e, N_EXPERTS - 1).astype(jnp.int32)
    n_used = cum[-1:].astype(jnp.int32)
    spare = nb * ROW_BLOCK
    rows = spare + ROW_BLOCK
    dest, gw = _dispatch_index(cmb, off_col, spare)
    idx = (dest[:, None, :] + (jnp.arange(slabs, dtype=jnp.int32) * rows)[None, :, None]).reshape(-1)
    xs = _sc_scatter_rows(n2p.reshape(slabs * m, words), idx, slabs * rows)
    ys = _expert_blocks(blk_e, n_used, xs.reshape(slabs, rows, words), w_gate, w_up, w_down)
    y = _sc_gather_rows(ys.reshape(slabs * rows, words), idx)
    return _combine(h, y.reshape(2, slabs, m, words), gw)


def _prepare_weights(p):
    d = p["w_in"].shape[0]
    splits = [Q_LORA, KV_LORA, QK_ROPE, 3 * RWKV_WIDTH + 4 * 64 + 128, d, d]
    offs = [0]
    for s in splits:
        offs.append(offs[-1] + s)
    w_in = p["w_in"].astype(BF16)
    zeros64 = jnp.zeros((64, RWKV_WIDTH), F32)

    def pad_lo(a):
        return jnp.concatenate([a, zeros64], axis=0).astype(BF16)

    def pad_hi(a):
        return jnp.concatenate([zeros64, a], axis=0).astype(BF16)

    row = lambda a: a.reshape(1, -1).astype(F32)
    wr = jnp.concatenate(
        [jnp.transpose(p["w_router_expert"], (1, 0, 2)).reshape(d, N_EXPERTS), p["w_router_group"],
         jnp.zeros((d, LANES - N_EXPERTS - N_GROUPS), F32)], axis=1)
    br = jnp.concatenate([p["b_router_expert"].reshape(-1), p["b_router_group"],
                          jnp.zeros((LANES - N_EXPERTS - N_GROUPS,), F32)]).reshape(1, LANES)
    return {
        "attn_norm": p["attn_norm"],
        "w_all": jnp.concatenate(
            [w_in[:, offs[3]:offs[4]], w_in[:, offs[0]:offs[3]],
             jnp.zeros((d, Z_MLA - (offs[3] - offs[0])), BF16), w_in[:, offs[4]:offs[6]]], axis=1),
        "qa": row(p["q_a_norm"]), "kva": row(p["kv_a_norm"]),
        "wq": p["w_q_b"].astype(BF16), "wkv": p["w_kv_b"].astype(BF16),
        "qn": row(p["q_norm"]), "kn": row(p["k_norm"]),
        "shared": {"sp": row(p["shift_prev"]), "sn": row(p["shift_next"]), "k_k": row(p["k_k"]),
                   "k_a": row(p["k_a"]), "r_k": row(p["r_k"])},
        "fwd": {"w0": row(p["decay_w0_fwd"]), "w2": pad_lo(p["decay_w2_fwd"]),
                "a0": row(p["icl_a0_fwd"]), "a2": pad_lo(p["icl_a2_fwd"])},
        "bwd": {"w0": row(p["decay_w0_bwd"]), "w2": pad_hi(p["decay_w2_bwd"]),
                "a0": row(p["icl_a0_bwd"]), "a2": pad_hi(p["icl_a2_bwd"])},
        "tail": {"g2": p["gate_w2"].astype(BF16), "ln_w": row(p["ln_x_w"]), "ln_b": row(p["ln_x_b"])},
        "w_a": p["w_branch_mla"].astype(BF16), "w_b": p["w_branch_rwkv"].astype(BF16),
        "w_out": p["w_out"].astype(BF16), "ffn_norm": row(p["ffn_norm"]),
        "wr": wr, "br": br,
        "w_gate": p["w_expert_gate"].astype(BF16), "w_up": p["w_expert_up"].astype(BF16),
        "w_down": p["w_expert_down"].astype(BF16),
    }


def _rope_tables(length):
    pos = jnp.arange(length, dtype=F32)
    inv = ROPE_THETA ** (-jnp.arange(0, QK_ROPE, 2, dtype=F32) / QK_ROPE)
    ang = pos[:, None] * inv[None, :]
    cos, sin = jnp.cos(ang), jnp.sin(ang)
    return jnp.concatenate([cos, cos], axis=-1), jnp.concatenate([-sin, sin], axis=-1)


def _trunk(x, meta_tokens, wts):
    batch, seq, d = x.shape
    m = batch * seq
    x2 = x.reshape(m, d)
    meta = meta_tokens.astype(F32)
    cos, sin = _rope_tables(seq + N_META)

    zm = _norm_matmul(meta, wts["attn_norm"], wts["w_all"], tn=Z_TILE)
    z = _norm_matmul(x2, wts["attn_norm"], wts["w_all"], tn=Z_TILE)

    mla_w = (wts["qa"], wts["kva"], wts["wq"], wts["wkv"], wts["qn"], wts["kn"])
    q, k, v = _mla_prep(z, batch, seq, cos[N_META:], sin[N_META:], *mla_w)
    _, km, vm = _mla_prep(zm, 1, N_META, cos[:N_META], sin[:N_META], *mla_w)
    pad = ((0, 0), (0, META_ROWS - N_META), (0, 0))
    o_a = _attention(q, k, v, jnp.pad(km[0], pad), jnp.pad(vm[0], pad))
    o_a = o_a.reshape(m, MLA_HEADS * V_HEAD)

    lead = jnp.concatenate([jnp.zeros((CHUNK - N_META, Z_RW), F32), zm[:, :Z_RW]], axis=0)
    o_b = _rwkv(z.reshape(batch, seq, -1), lead, wts["shared"], wts["fwd"], wts["bwd"],
                wts["tail"])
    o_b = o_b.reshape(m, RWKV_WIDTH)

    h, n2, cmb = _out_router(x2, o_a, o_b, z, wts["w_a"], wts["w_b"], wts["w_out"],
                             wts["ffn_norm"], wts["wr"], wts["br"])
    out = _moe_dispatch(n2, h, cmb, wts["w_gate"], wts["w_up"], wts["w_down"])
    return out.reshape(batch, seq, d)


def kernel(x_prompt, x_sample, meta_tokens, attn_norm, w_in, q_a_norm, w_q_b, kv_a_norm, w_kv_b, q_norm, k_norm, shift_prev, shift_next, decay_w0_fwd, decay_w2_fwd, decay_w0_bwd, decay_w2_bwd, icl_a0_fwd, icl_a2_fwd, icl_a0_bwd, icl_a2_bwd, gate_w2, k_k, k_a, r_k, ln_x_w, ln_x_b, w_branch_mla, w_branch_rwkv, w_out, ffn_norm, w_router_group, b_router_group, w_router_expert, b_router_expert, w_expert_gate, w_expert_up, w_expert_down):
    p = {
        "attn_norm": attn_norm, "w_in": w_in, "q_a_norm": q_a_norm, "w_q_b": w_q_b,
        "kv_a_norm": kv_a_norm, "w_kv_b": w_kv_b, "q_norm": q_norm, "k_norm": k_norm,
        "shift_prev": shift_prev, "shift_next": shift_next,
        "decay_w0_fwd": decay_w0_fwd, "decay_w2_fwd": decay_w2_fwd,
        "decay_w0_bwd": decay_w0_bwd, "decay_w2_bwd": decay_w2_bwd,
        "icl_a0_fwd": icl_a0_fwd, "icl_a2_fwd": icl_a2_fwd,
        "icl_a0_bwd": icl_a0_bwd, "icl_a2_bwd": icl_a2_bwd,
        "gate_w2": gate_w2, "k_k": k_k, "k_a": k_a, "r_k": r_k, "ln_x_w": ln_x_w, "ln_x_b": ln_x_b,
        "w_branch_mla": w_branch_mla, "w_branch_rwkv": w_branch_rwkv, "w_out": w_out,
        "ffn_norm": ffn_norm, "w_router_group": w_router_group, "b_router_group": b_router_group,
        "w_router_expert": w_router_expert, "b_router_expert": b_router_expert,
        "w_expert_gate": w_expert_gate, "w_expert_up": w_expert_up, "w_expert_down": w_expert_down,
    }
    wts = _prepare_weights({name: val[0] for name, val in p.items()})
    return _trunk(x_prompt, meta_tokens, wts), _trunk(x_sample, meta_tokens, wts)
```
